```python
import math
import jax, jax.numpy as jnp
from jax import lax
import numpy as np

D_MODEL = 2048
BATCH = 1
SEQ = 8192
DEPTH = 4
DEC_BATCH = 4
DEC_SEQ = 8192
PAST_LEN = 128

D_LRU = D_MODEL // 2
LRU_BLOCKS = 8
LRU_BLOCK = D_LRU // LRU_BLOCKS
CONV_WIDTH = 4
CONV_PAD = (2, 1)
LRU_C = 8.0
V_HEAD_DIM = 128
D_ATT = D_MODEL - D_LRU
N_HEADS = D_ATT // V_HEAD_DIM
QK_NOPE = 128
QK_ROPE = 64
QK_HEAD = QK_NOPE + QK_ROPE
Q_LORA = D_MODEL // 4
KV_LORA = D_MODEL // 8
ROPE_BASE = 10000.0
Q_BLOCK = 128
D_IN = 2 * D_LRU + Q_LORA + KV_LORA + QK_ROPE
N_EXPERTS = 16
D_EXPERT = D_MODEL
CAPACITY = 2
EPS = 1e-6

kernel_name = "hymba_rglru_mla_expert_choice_encoder"

F32 = jnp.float32


def rms_norm(x, g):
    xf = x.astype(F32)
    y = xf * lax.rsqrt(jnp.mean(xf * xf, axis=-1, keepdims=True) + EPS)
    return (y * g.astype(F32)).astype(x.dtype)


def rope_tables(s):
    pos = jnp.arange(s, dtype=F32)
    inv = ROPE_BASE ** (-jnp.arange(0, QK_ROPE, 2, dtype=F32) / QK_ROPE)
    ang = pos[:, None] * inv[None, :]
    return jnp.cos(ang)[None, :, None, :], jnp.sin(ang)[None, :, None, :]


def rope_tail(t, cos, sin):
    nope = t[..., :QK_NOPE]
    r = t[..., QK_NOPE:].astype(F32)
    r1, r2 = r[..., :QK_ROPE // 2], r[..., QK_ROPE // 2:]
    rot = jnp.concatenate([r1 * cos - r2 * sin, r2 * cos + r1 * sin], axis=-1)
    return jnp.concatenate([nope, rot.astype(t.dtype)], axis=-1)


def centred_depthwise_conv(x, w, b):
    c = x.shape[-1]
    y = lax.conv_general_dilated(
        x, w.reshape(CONV_WIDTH, 1, c), window_strides=(1,), padding=[CONV_PAD],
        dimension_numbers=("NWC", "WIO", "NWC"), feature_group_count=c)
    return y + b


def _linear_combine(left, right):
    a_l, b_l = left
    a_r, b_r = right
    return a_l * a_r, a_r * b_l + b_r


def rg_lru(xc, w_a, b_a, w_x, b_x, lam, reverse):
    bsz, s, _ = xc.shape
    xb = xc.reshape(bsz, s, LRU_BLOCKS, LRU_BLOCK)
    r = jax.nn.sigmoid((jnp.einsum("bsnd,nde->bsne", xb, w_a).reshape(bsz, s, D_LRU) + b_a).astype(F32))
    i = jax.nn.sigmoid((jnp.einsum("bsnd,nde->bsne", xb, w_x).reshape(bsz, s, D_LRU) + b_x).astype(F32))
    log_a = LRU_C * r * jax.nn.log_sigmoid(lam.astype(F32))
    a = jnp.exp(log_a)
    u = jnp.sqrt(-jnp.expm1(2.0 * log_a)) * (i * xc.astype(F32))
    _, h = lax.associative_scan(_linear_combine, (a, u), axis=1, reverse=reverse)
    return h


def mla_attention(c_q, c_kv, k_rope, q_norm, w_q_up, kv_norm, w_kv_up, q_head_norm, k_head_norm):
    bsz, s, _ = c_q.shape
    q = (rms_norm(c_q, q_norm) @ w_q_up).reshape(bsz, s, N_HEADS, QK_HEAD)
    kv = (rms_norm(c_kv, kv_norm) @ w_kv_up).reshape(bsz, s, N_HEADS, QK_NOPE + V_HEAD_DIM)
    k_nope, v = kv[..., :QK_NOPE], kv[..., QK_NOPE:]
    k_r = jnp.broadcast_to(k_rope[:, :, None, :], (bsz, s, N_HEADS, QK_ROPE))
    k = jnp.concatenate([k_nope, k_r], axis=-1)
    q = rms_norm(q, q_head_norm)
    k = rms_norm(k, k_head_norm)
    cos, sin = rope_tables(s)
    q = rope_tail(q, cos, sin)
    k = rope_tail(k, cos, sin)
    q32 = q.astype(F32) * (QK_HEAD ** -0.5)
    k32 = k.astype(F32)
    nb = s // Q_BLOCK
    qb = q32.reshape(bsz, nb, Q_BLOCK, N_HEADS, QK_HEAD).transpose(1, 0, 2, 3, 4)

    def attend(qblk):
        sc = jnp.einsum("bqhd,bkhd->bhqk", qblk, k32)
        p = jax.nn.softmax(sc, axis=-1).astype(v.dtype)
        return jnp.einsum("bhqk,bkhd->bqhd", p, v)

    o = lax.map(attend, qb)
    return o.transpose(1, 0, 2, 3, 4).reshape(bsz, s, N_HEADS * V_HEAD_DIM)


def expert_choice_ffn(x, w_router, w1, w3, w2):
    bsz, s, d = x.shape
    n = bsz * s
    cap = max(1, (CAPACITY * n) // N_EXPERTS)
    xt = x.reshape(n, d)
    aff = jax.nn.softmax((xt @ w_router).astype(F32), axis=-1)
    gate, idx = lax.top_k(aff.T, cap)
    xe = xt[idx]
    hid = jax.nn.silu(jnp.einsum("ecd,edf->ecf", xe, w1)) * jnp.einsum("ecd,edf->ecf", xe, w3)
    ye = jnp.einsum("ecf,efd->ecd", hid, w2) * gate[..., None].astype(x.dtype)
    out = jnp.zeros((n, d), x.dtype).at[idx.reshape(-1)].add(ye.reshape(-1, d))
    return out.reshape(bsz, s, d)


def encoder_layer(x, ln1, w_in, conv_w, conv_b, lru_w_a, lru_b_a, lru_w_x, lru_b_x, lru_lam,
                  q_norm, w_q_up, kv_norm, w_kv_up, q_head_norm, k_head_norm, w_out,
                  ln2, w_router, w1, w3, w2):
    h = rms_norm(x, ln1) @ w_in
    s0 = D_LRU
    s1 = 2 * D_LRU
    s2 = s1 + Q_LORA
    s3 = s2 + KV_LORA
    x_rnn, y_gate = h[..., :s0], h[..., s0:s1]
    c_q, c_kv, k_rope = h[..., s1:s2], h[..., s2:s3], h[..., s3:]
    xc = centred_depthwise_conv(x_rnn, conv_w, conv_b)
    h_f = rg_lru(xc, lru_w_a[0], lru_b_a[0], lru_w_x[0], lru_b_x[0], lru_lam[0], reverse=False)
    h_b = rg_lru(xc, lru_w_a[1], lru_b_a[1], lru_w_x[1], lru_b_x[1], lru_lam[1], reverse=True)
    rnn_out = ((h_f + h_b) * jax.nn.gelu(y_gate.astype(F32))).astype(x.dtype)
    att_out = mla_attention(c_q, c_kv, k_rope, q_norm, w_q_up, kv_norm, w_kv_up, q_head_norm, k_head_norm)
    x = x + jnp.concatenate([rnn_out, att_out], axis=-1) @ w_out
    x = x + expert_choice_ffn(rms_norm(x, ln2), w_router, w1, w3, w2)
    return x


def setup_inputs(seed: int = 0) -> dict:
    key = jax.random.key(seed)
    ks = jax.random.split(key, 24)
    nrm = jax.random.normal

    def gain(k, shape):
        return 1.0 + 0.02 * nrm(k, shape, F32)

    a0 = jax.random.uniform(ks[11], (DEPTH, 2, D_LRU), F32, minval=0.9, maxval=0.999)
    return {
        "x_prompt": nrm(ks[0], (BATCH, SEQ, D_MODEL), F32),
        "x_sample": nrm(ks[1], (DEC_BATCH, DEC_SEQ, D_MODEL), F32),
        "ln1": gain(ks[2], (DEPTH, D_MODEL)),
        "w_in": nrm(ks[3], (DEPTH, D_MODEL, D_IN), F32) * D_MODEL ** -0.5,
        "conv_w": nrm(ks[4], (DEPTH, CONV_WIDTH, D_LRU), F32) * CONV_WIDTH ** -0.5,
        "conv_b": 0.01 * nrm(ks[5], (DEPTH, D_LRU), F32),
        "lru_w_a": nrm(ks[6], (DEPTH, 2, LRU_BLOCKS, LRU_BLOCK, LRU_BLOCK), F32) * LRU_BLOCK ** -0.5,
        "lru_b_a": 0.01 * nrm(ks[7], (DEPTH, 2, D_LRU), F32),
        "lru_w_x": nrm(ks[8], (DEPTH, 2, LRU_BLOCKS, LRU_BLOCK, LRU_BLOCK), F32) * LRU_BLOCK ** -0.5,
        "lru_b_x": 0.01 * nrm(ks[9], (DEPTH, 2, D_LRU), F32),
        "lru_lam": jnp.log(a0) - jnp.log1p(-a0),
        "q_norm": gain(ks[12], (DEPTH, Q_LORA)),
        "w_q_up": nrm(ks[13], (DEPTH, Q_LORA, N_HEADS * QK_HEAD), F32) * Q_LORA ** -0.5,
        "kv_norm": gain(ks[14], (DEPTH, KV_LORA)),
        "w_kv_up": nrm(ks[15], (DEPTH, KV_LORA, N_HEADS * (QK_NOPE + V_HEAD_DIM)), F32) * KV_LORA ** -0.5,
        "q_head_norm": gain(ks[16], (DEPTH, QK_HEAD)),
        "k_head_norm": gain(ks[17], (DEPTH, QK_HEAD)),
        "w_out": nrm(ks[18], (DEPTH, D_LRU + D_ATT, D_MODEL), F32) * (D_LRU + D_ATT) ** -0.5,
        "ln2": gain(ks[19], (DEPTH, D_MODEL)),
        "w_router": nrm(ks[20], (DEPTH, D_MODEL, N_EXPERTS), F32) * D_MODEL ** -0.5,
        "w1": nrm(ks[21], (DEPTH, N_EXPERTS, D_MODEL, D_EXPERT), F32) * D_MODEL ** -0.5,
        "w3": nrm(ks[22], (DEPTH, N_EXPERTS, D_MODEL, D_EXPERT), F32) * D_MODEL ** -0.5,
        "w2": nrm(ks[23], (DEPTH, N_EXPERTS, D_EXPERT, D_MODEL), F32) * D_EXPERT ** -0.5,
        "final_norm": gain(ks[10], (D_MODEL,)),
    }


def reference(x_prompt, x_sample, ln1, w_in, conv_w, conv_b, lru_w_a, lru_b_a, lru_w_x, lru_b_x,
              lru_lam, q_norm, w_q_up, kv_norm, w_kv_up, q_head_norm, k_head_norm, w_out, ln2,
              w_router, w1, w3, w2, final_norm):
    xp = x_prompt
    xs = x_sample
    for l in range(DEPTH):
        lp = (ln1[l], w_in[l], conv_w[l], conv_b[l], lru_w_a[l], lru_b_a[l], lru_w_x[l], lru_b_x[l],
              lru_lam[l], q_norm[l], w_q_up[l], kv_norm[l], w_kv_up[l], q_head_norm[l],
              k_head_norm[l], w_out[l], ln2[l], w_router[l], w1[l], w3[l], w2[l])
        xp = encoder_layer(xp, *lp)
        xs = encoder_layer(xs, *lp)
    y_prompt = rms_norm(xp, final_norm)
    y_sample = rms_norm(xs, final_norm)
    return (y_prompt, y_sample)
```

```python
import functools
import math

import jax
import jax.numpy as jnp
from jax import lax
from jax.experimental import pallas as pl
from jax.experimental.pallas import tpu as pltpu

F32 = jnp.float32
BF16 = jnp.bfloat16

EPS = 1e-6
LRU_BLOCKS = 8
CONV_WIDTH = 4
CONV_PAD = (2, 1)
LRU_C = 8.0
V_HEAD_DIM = 128
QK_NOPE = 128
QK_ROPE = 64
QK_HEAD = QK_NOPE + QK_ROPE
QK_PAD = 256
ROPE_BASE = 10000.0
N_EXPERTS = 16
CAPACITY = 2

V7X_VMEM_LIMIT_BYTES = 56 * 1024 * 1024


def _tile(n, pref):
    t = min(n, pref)
    while n % t:
        t //= 2
    return t


def _params(*semantics):
    return pltpu.CompilerParams(dimension_semantics=semantics,
                                vmem_limit_bytes=V7X_VMEM_LIMIT_BYTES)


def _norm_matmul_kernel(x_ref, g_ref, w_ref, o_ref):
    x = x_ref[...]
    y = x * lax.rsqrt(jnp.mean(x * x, axis=-1, keepdims=True) + EPS) * g_ref[...]
    o_ref[...] = jnp.dot(y.astype(BF16), w_ref[...], preferred_element_type=F32)


def norm_matmul(x, g, w_bf16, tm):
    rows, d = x.shape
    n = w_bf16.shape[1]
    return pl.pallas_call(
        _norm_matmul_kernel,
        out_shape=jax.ShapeDtypeStruct((rows, n), F32),
        grid=(rows // tm,),
        in_specs=[pl.BlockSpec((tm, d), lambda i: (i, 0)),
                  pl.BlockSpec((1, d), lambda i: (0, 0)),
                  pl.BlockSpec((d, n), lambda i: (0, 0))],
        out_specs=pl.BlockSpec((tm, n), lambda i: (i, 0)),
        compiler_params=_params("parallel"),
        name="norm_matmul",
    )(x, g.reshape(1, d), w_bf16)


def _residual_matmul_kernel(a_ref, w_ref, x_ref, o_ref):
    o_ref[...] = x_ref[...] + jnp.dot(a_ref[...].astype(BF16), w_ref[...],
                                      preferred_element_type=F32)


def residual_matmul(a, w_bf16, x, tm):
    rows, k = a.shape
    n = w_bf16.shape[1]
    return pl.pallas_call(
        _residual_matmul_kernel,
        out_shape=jax.ShapeDtypeStruct((rows, n), F32),
        grid=(rows // tm,),
        in_specs=[pl.BlockSpec((tm, k), lambda i: (i, 0)),
                  pl.BlockSpec((k, n), lambda i: (0, 0)),
                  pl.BlockSpec((tm, n), lambda i: (i, 0))],
        out_specs=pl.BlockSpec((tm, n), lambda i: (i, 0)),
        compiler_params=_params("parallel"),
        name="residual_matmul",
    )(a, w_bf16, x)


def _flash_kernel(q_ref, k_ref, v_ref, o_ref, m_sc, l_sc, acc_sc, *, tk):
    s_len = k_ref.shape[1]
    q = q_ref[0]
    m_sc[...] = jnp.full(m_sc.shape, -jnp.inf, F32)
    l_sc[...] = jnp.zeros(l_sc.shape, F32)
    acc_sc[...] = jnp.zeros(acc_sc.shape, F32)

    def body(j, carry):
        start = pl.multiple_of(j * tk, tk)
        k = k_ref[0, pl.ds(start, tk), :]
        v = v_ref[0, pl.ds(start, tk), :]
        s = lax.dot_general(q, k, (((1,), (1,)), ((), ())),
                            preferred_element_type=F32)
        m_prev = m_sc[...]
        m_new = jnp.maximum(m_prev, jnp.max(s, axis=-1, keepdims=True))
        alpha = jnp.exp(m_prev - m_new)
        p = jnp.exp(s - m_new[:, :1])
        l_sc[...] = alpha * l_sc[...] + jnp.sum(p, axis=-1, keepdims=True)
        acc_sc[...] = alpha * acc_sc[...] + jnp.dot(
            p.astype(BF16), v, preferred_element_type=F32)
        m_sc[...] = m_new
        return carry

    lax.fori_loop(0, s_len // tk, body, 0)
    o_ref[0] = (acc_sc[...] / l_sc[...]).astype(o_ref.dtype)


def flash_attention(q, k, v, n_heads, tq, tk):
    g, s_len, dq = q.shape
    dv = v.shape[-1]
    n_seq = g // n_heads
    return pl.pallas_call(
        functools.partial(_flash_kernel, tk=tk),
        out_shape=jax.ShapeDtypeStruct((n_seq, s_len, n_heads * dv), BF16),
        grid=(g, s_len // tq),
        in_specs=[pl.BlockSpec((1, tq, dq), lambda gi, i: (gi, i, 0)),
                  pl.BlockSpec((1, s_len, dq), lambda gi, i: (gi, 0, 0)),
                  pl.BlockSpec((1, s_len, dv), lambda gi, i: (gi, 0, 0))],
        out_specs=pl.BlockSpec((1, tq, dv),
                               lambda gi, i: (gi // n_heads, i, gi % n_heads)),
        scratch_shapes=[pltpu.VMEM((tq, dv), F32),
                        pltpu.VMEM((tq, dv), F32),
                        pltpu.VMEM((tq, dv), F32)],
        compiler_params=_params("parallel", "arbitrary"),
        name="flash_attention",
    )(q, k, v)


def _ffn_kernel(xe_ref, gate_ref, w1_ref, w3_ref, w2_ref, o_ref):
    f = pl.program_id(1)
    x = xe_ref[0]
    a = jnp.dot(x, w1_ref[0], preferred_element_type=F32)
    b = jnp.dot(x, w3_ref[0], preferred_element_type=F32)
    hid = (a * jax.nn.sigmoid(a)) * b
    y = jnp.dot(hid.astype(BF16), w2_ref[0], preferred_element_type=F32)

    @pl.when(f == 0)
    def _():
        o_ref[0] = y

    @pl.when(f > 0)
    def _():
        o_ref[0] += y

    @pl.when(f == pl.num_programs(1) - 1)
    def _():
        o_ref[0] = o_ref[0] * gate_ref[0]


def expert_ffn(xe, gate, w1, w3, w2, tiles_per_expert, tf):
    n_tiles, tm, d = xe.shape
    d_ff = w1.shape[-1]
    e_of = lambda t: t // tiles_per_expert
    return pl.pallas_call(
        _ffn_kernel,
        out_shape=jax.ShapeDtypeStruct((n_tiles, tm, d), F32),
        grid=(n_tiles, d_ff // tf),
        in_specs=[pl.BlockSpec((1, tm, d), lambda t, f: (t, 0, 0)),
                  pl.BlockSpec((1, tm, 1), lambda t, f: (t, 0, 0)),
                  pl.BlockSpec((1, d, tf), lambda t, f: (e_of(t), 0, f)),
                  pl.BlockSpec((1, d, tf), lambda t, f: (e_of(t), 0, f)),
                  pl.BlockSpec((1, tf, d), lambda t, f: (e_of(t), f, 0))],
        out_specs=pl.BlockSpec((1, tm, d), lambda t, f: (t, 0, 0)),
        compiler_params=_params("parallel", "arbitrary"),
        name="expert_ffn",
    )(xe, gate, w1, w3, w2)


def _rms_norm(x, g):
    xf = x.astype(F32)
    y = xf * lax.rsqrt(jnp.mean(xf * xf, axis=-1, keepdims=True) + EPS)
    return y * g.astype(F32)


def _rope_tables(s):
    pos = jnp.arange(s, dtype=F32)
    inv = ROPE_BASE ** (-jnp.arange(0, QK_ROPE, 2, dtype=F32) / QK_ROPE)
    ang = pos[:, None] * inv[None, :]
    return jnp.cos(ang)[None, :, None, :], jnp.sin(ang)[None, :, None, :]


def _rope_tail(t, cos, sin):
    nope = t[..., :QK_NOPE]
    r = t[..., QK_NOPE:]
    r1, r2 = r[..., :QK_ROPE // 2], r[..., QK_ROPE // 2:]
    rot = jnp.concatenate([r1 * cos - r2 * sin, r2 * cos + r1 * sin], axis=-1)
    return jnp.concatenate([nope, rot], axis=-1)


def _linear_combine(left, right):
    a_l, b_l = left
    a_r, b_r = right
    return a_l * a_r, a_r * b_l + b_r


def _rg_lru(xc, w_a, b_a, w_x, b_x, lam, reverse):
    bsz, s, d_lru = xc.shape
    blk = d_lru // LRU_BLOCKS
    xb = xc.reshape(bsz, s, LRU_BLOCKS, blk)
    r = jax.nn.sigmoid(jnp.einsum("bsnd,nde->bsne", xb, w_a).reshape(bsz, s, d_lru) + b_a)
    i = jax.nn.sigmoid(jnp.einsum("bsnd,nde->bsne", xb, w_x).reshape(bsz, s, d_lru) + b_x)
    log_a = LRU_C * r * jax.nn.log_sigmoid(lam)
    a = jnp.exp(log_a)
    u = jnp.sqrt(-jnp.expm1(2.0 * log_a)) * (i * xc)
    _, h = lax.associative_scan(_linear_combine, (a, u), axis=1, reverse=reverse)
    return h


def _conv(x, w, b):
    c = x.shape[-1]
    y = lax.conv_general_dilated(
        x, w.reshape(CONV_WIDTH, 1, c), window_strides=(1,), padding=[CONV_PAD],
        dimension_numbers=("NWC", "WIO", "NWC"), feature_group_count=c)
    return y + b


def _layer(x, n_prompt_seq, p):
    n_seq, s_len, d = x.shape
    rows = n_seq * s_len
    d_lru = p["conv_w"].shape[-1]
    q_lora = p["q_norm"].shape[-1]
    kv_lora = p["kv_norm"].shape[-1]
    n_heads = p["w_q_up"].shape[-1] // QK_HEAD
    s1 = 2 * d_lru
    s2 = s1 + q_lora
    s3 = s2 + kv_lora

    h = norm_matmul(x.reshape(rows, d), p["ln1"], p["w_in"], tm=_tile(rows, 256))
    h = h.reshape(n_seq, s_len, -1)
    x_rnn, y_gate = h[..., :d_lru], h[..., d_lru:s1]
    c_q, c_kv, k_rope = h[..., s1:s2], h[..., s2:s3], h[..., s3:]

    xc = _conv(x_rnn, p["conv_w"], p["conv_b"])
    h_f = _rg_lru(xc, p["lru_w_a"][0], p["lru_b_a"][0], p["lru_w_x"][0], p["lru_b_x"][0],
                  p["lru_lam"][0], reverse=False)
    h_b = _rg_lru(xc, p["lru_w_a"][1], p["lru_b_a"][1], p["lru_w_x"][1], p["lru_b_x"][1],
                  p["lru_lam"][1], reverse=True)
    rnn_out = (h_f + h_b) * jax.nn.gelu(y_gate)

    q = (_rms_norm(c_q, p["q_norm"]) @ p["w_q_up"]).reshape(n_seq, s_len, n_heads, QK_HEAD)
    kv = (_rms_norm(c_kv, p["kv_norm"]) @ p["w_kv_up"]).reshape(
        n_seq, s_len, n_heads, QK_NOPE + V_HEAD_DIM)
    k_nope, v = kv[..., :QK_NOPE], kv[..., QK_NOPE:]
    k_r = jnp.broadcast_to(k_rope[:, :, None, :], (n_seq, s_len, n_heads, QK_ROPE))
    k = jnp.concatenate([k_nope, k_r], axis=-1)
    q = _rms_norm(q, p["q_head_norm"])
    k = _rms_norm(k, p["k_head_norm"])
    cos, sin = _rope_tables(s_len)
    q = _rope_tail(q, cos, sin) * (QK_HEAD ** -0.5)
    k = _rope_tail(k, cos, sin)
    pad = ((0, 0), (0, 0), (0, 0), (0, QK_PAD - QK_HEAD))
    qh = jnp.pad(q, pad).astype(BF16).transpose(0, 2, 1, 3).reshape(n_seq * n_heads, s_len, QK_PAD)
    kh = jnp.pad(k, pad).astype(BF16).transpose(0, 2, 1, 3).reshape(n_seq * n_heads, s_len, QK_PAD)
    vh = v.astype(BF16).transpose(0, 2, 1, 3).reshape(n_seq * n_heads, s_len, V_HEAD_DIM)
    att = flash_attention(qh, kh, vh, n_heads, tq=_tile(s_len, 1024), tk=_tile(s_len, 1024))

    mix = jnp.concatenate([rnn_out.astype(BF16), att], axis=-1).reshape(rows, -1)
    x1 = residual_matmul(mix, p["w_out"], x.reshape(rows, d), tm=_tile(rows, 512))

    xn = _rms_norm(x1, p["ln2"])
    logits = jnp.dot(xn, p["w_router"], precision=lax.Precision.HIGHEST)
    aff = jax.nn.softmax(logits, axis=-1)
    xn_bf = xn.astype(BF16)

    outs = []
    for lo, hi in ((0, n_prompt_seq * s_len), (n_prompt_seq * s_len, rows)):
        n = hi - lo
        cap = max(1, (CAPACITY * n) // N_EXPERTS)
        gate, idx = lax.top_k(aff[lo:hi].T, cap)
        xe = xn_bf[lo:hi][idx]
        tm = _tile(cap, 1024)
        tiles_per_expert = cap // tm
        ye = expert_ffn(xe.reshape(N_EXPERTS * tiles_per_expert, tm, d),
                        gate.reshape(N_EXPERTS * tiles_per_expert, tm, 1),
                        p["w1"], p["w3"], p["w2"], tiles_per_expert, tf=_tile(d, 512))
        out = jnp.zeros((n, d), F32).at[idx.reshape(-1)].add(ye.reshape(-1, d))
        outs.append(x1[lo:hi] + out)
    return jnp.concatenate(outs, axis=0).reshape(n_seq, s_len, d)


def kernel(x_prompt, x_sample, ln1, w_in, conv_w, conv_b, lru_w_a, lru_b_a, lru_w_x, lru_b_x,
           lru_lam, q_norm, w_q_up, kv_norm, w_kv_up, q_head_norm, k_head_norm, w_out, ln2,
           w_router, w1, w3, w2, final_norm):
    depth = ln1.shape[0]
    n_prompt_seq = x_prompt.shape[0]
    x = jnp.concatenate([x_prompt, x_sample], axis=0)
    for l in range(depth):
        p = dict(ln1=ln1[l], w_in=w_in[l].astype(BF16), conv_w=conv_w[l], conv_b=conv_b[l],
                 lru_w_a=lru_w_a[l], lru_b_a=lru_b_a[l], lru_w_x=lru_w_x[l], lru_b_x=lru_b_x[l],
                 lru_lam=lru_lam[l], q_norm=q_norm[l], w_q_up=w_q_up[l], kv_norm=kv_norm[l],
                 w_kv_up=w_kv_up[l], q_head_norm=q_head_norm[l], k_head_norm=k_head_norm[l],
                 w_out=w_out[l].astype(BF16), ln2=ln2[l], w_router=w_router[l],
                 w1=w1[l].astype(BF16), w3=w3[l].astype(BF16), w2=w2[l].astype(BF16))
        x = _layer(x, n_prompt_seq, p)
    y = _rms_norm(x, final_norm)
    return y[:n_prompt_seq], y[n_prompt_seq:]
```

```python
import functools
import math

import jax
import jax.numpy as jnp
from jax import lax
from jax.experimental import pallas as pl
from jax.experimental.pallas import tpu as pltpu

F32 = jnp.float32
BF16 = jnp.bfloat16

EPS = 1e-6
LRU_BLOCKS = 8
CONV_WIDTH = 4
LRU_C = 8.0
V_HEAD_DIM = 128
QK_NOPE = 128
QK_ROPE = 64
QK_HEAD = QK_NOPE + QK_ROPE
ROPE_BASE = 10000.0
N_EXPERTS = 16
CAPACITY = 2

LANES = 128
SUBLANES = 8
HEAD_PAD = 2 * LANES
V7X_VMEM_LIMIT_BYTES = 56 * 1024 * 1024
LOG2E = math.log2(math.e)


def _tile(n, pref):
    t = min(n, pref)
    while n % t:
        t //= 2
    return t


def _params(*semantics):
    return pltpu.CompilerParams(dimension_semantics=semantics,
                                vmem_limit_bytes=V7X_VMEM_LIMIT_BYTES)


def _rms(x):
    return lax.rsqrt(jnp.mean(x * x, axis=-1, keepdims=True) + EPS)


def _norm_matmul_kernel(x_ref, g_ref, w_ref, o_ref):
    x = x_ref[...]
    y = x * _rms(x) * g_ref[...]
    o_ref[...] = jnp.dot(y.astype(BF16), w_ref[...], preferred_element_type=F32)


def norm_matmul(x, g, w_bf16, tm):
    rows, d = x.shape
    n = w_bf16.shape[1]
    return pl.pallas_call(
        _norm_matmul_kernel,
        out_shape=jax.ShapeDtypeStruct((rows, n), F32),
        grid=(rows // tm,),
        in_specs=[pl.BlockSpec((tm, d), lambda i: (i, 0)),
                  pl.BlockSpec((1, d), lambda i: (0, 0)),
                  pl.BlockSpec((d, n), lambda i: (0, 0))],
        out_specs=pl.BlockSpec((tm, n), lambda i: (i, 0)),
        compiler_params=_params("parallel"),
        name="norm_matmul",
    )(x, g.reshape(1, d), w_bf16)


def _norm_kernel(x_ref, g_ref, o_ref):
    x = x_ref[...]
    o_ref[...] = x * _rms(x) * g_ref[...]


def rms_norm_rows(x, g, tm):
    rows, d = x.shape
    return pl.pallas_call(
        _norm_kernel,
        out_shape=jax.ShapeDtypeStruct((rows, d), F32),
        grid=(rows // tm,),
        in_specs=[pl.BlockSpec((tm, d), lambda i: (i, 0)),
                  pl.BlockSpec((1, d), lambda i: (0, 0))],
        out_specs=pl.BlockSpec((tm, d), lambda i: (i, 0)),
        compiler_params=_params("parallel"),
        name="final_norm",
    )(x, g.reshape(1, d))


def _lru_kernel(x_ref, prev_ref, next_ref, cw_ref, cb_ref, wg_ref, ba_ref, bx_ref, lam_ref,
                o_ref, ext_sc, a_sc, u_sc, carry_sc, *, reverse):
    c = pl.program_id(1)
    n_c = pl.num_programs(1)
    t_len, ch = a_sc.shape
    blk = ch // LRU_BLOCKS
    first_in_time = (c == n_c - 1) if reverse else (c == 0)
    last_in_time = (c == 0) if reverse else (c == n_c - 1)

    @pl.when(c == 0)
    def _():
        carry_sc[...] = jnp.zeros(carry_sc.shape, F32)

    ext_sc[0:SUBLANES, :] = jnp.where(first_in_time, 0.0, prev_ref[0])
    ext_sc[SUBLANES:SUBLANES + t_len, :] = x_ref[0]
    ext_sc[SUBLANES + t_len:, :] = jnp.where(last_in_time, 0.0, next_ref[0])
    xc = cb_ref[...]
    for tap in range(CONV_WIDTH):
        off = SUBLANES + tap - 2
        xc = xc + cw_ref[tap:tap + 1, :] * ext_sc[off:off + t_len, :]

    lam = lam_ref[...]
    log_sig_lam = jnp.minimum(lam, 0.0) - jnp.log1p(jnp.exp(-jnp.abs(lam)))
    for n in range(LRU_BLOCKS):
        sl = slice(n * blk, (n + 1) * blk)
        xcn = xc[:, sl]
        g = jnp.dot(xcn.astype(BF16), wg_ref[n], preferred_element_type=F32)
        r = jax.nn.sigmoid(g[:, :blk] + ba_ref[:, sl])
        i = jax.nn.sigmoid(g[:, blk:] + bx_ref[:, sl])
        log_a = LRU_C * r * log_sig_lam[:, sl]
        a_sc[:, sl] = jnp.exp(log_a)
        u_sc[:, sl] = jnp.sqrt(1.0 - jnp.exp(2.0 * log_a)) * (i * xcn)

    row = lax.broadcasted_iota(jnp.int32, (SUBLANES, ch), 0)
    n_groups = t_len // SUBLANES

    def group(gi, carry):
        g0 = (n_groups - 1 - gi) if reverse else gi
        start = pl.multiple_of(g0 * SUBLANES, SUBLANES)
        a = a_sc[pl.ds(start, SUBLANES), :]
        u = u_sc[pl.ds(start, SUBLANES), :]
        for d in (1, 2, 4):
            shift = SUBLANES - d if reverse else d
            a_s = pltpu.roll(a, shift, 0)
            u_s = pltpu.roll(u, shift, 0)
            valid = (row < SUBLANES - d) if reverse else (row >= d)
            u = jnp.where(valid, a * u_s + u, u)
            a = jnp.where(valid, a * a_s, a)
        h = u + a * carry
        o_ref[0, pl.ds(start, SUBLANES), :] = h
        edge = h[0:1, :] if reverse else h[SUBLANES - 1:SUBLANES, :]
        return jnp.broadcast_to(edge, (SUBLANES, ch))

    carry_sc[...] = lax.fori_loop(0, n_groups, group, carry_sc[...], unroll=2)


def lru_scan(h3, conv_w, conv_b, w_gate, b_a, b_x, lam, t_len, reverse):
    n_seq, s_len, _ = h3.shape
    ch = conv_w.shape[-1]
    n_c = s_len // t_len
    per = t_len // SUBLANES
    n_sub = s_len // SUBLANES
    cidx = (lambda c: n_c - 1 - c) if reverse else (lambda c: c)
    return pl.pallas_call(
        functools.partial(_lru_kernel, reverse=reverse),
        out_shape=jax.ShapeDtypeStruct((n_seq, s_len, ch), F32),
        grid=(n_seq, n_c),
        in_specs=[
            pl.BlockSpec((1, t_len, ch), lambda s, c: (s, cidx(c), 0)),
            pl.BlockSpec((1, SUBLANES, ch),
                         lambda s, c: (s, jnp.maximum(cidx(c) * per - 1, 0), 0)),
            pl.BlockSpec((1, SUBLANES, ch),
                         lambda s, c: (s, jnp.minimum((cidx(c) + 1) * per, n_sub - 1), 0)),
            pl.BlockSpec((CONV_WIDTH, ch), lambda s, c: (0, 0)),
            pl.BlockSpec((1, ch), lambda s, c: (0, 0)),
            pl.BlockSpec(w_gate.shape, lambda s, c: (0, 0, 0)),
            pl.BlockSpec((1, ch), lambda s, c: (0, 0)),
            pl.BlockSpec((1, ch), lambda s, c: (0, 0)),
            pl.BlockSpec((1, ch), lambda s, c: (0, 0)),
        ],
        out_specs=pl.BlockSpec((1, t_len, ch), lambda s, c: (s, cidx(c), 0)),
        scratch_shapes=[pltpu.VMEM((t_len + 2 * SUBLANES, ch), F32),
                        pltpu.VMEM((t_len, ch), F32),
                        pltpu.VMEM((t_len, ch), F32),
                        pltpu.VMEM((SUBLANES, ch), F32)],
        compiler_params=_params("parallel", "arbitrary"),
        name="lru_rev" if reverse else "lru_fwd",
    )(h3, h3, h3, conv_w, conv_b.reshape(1, ch), w_gate, b_a.reshape(1, ch),
      b_x.reshape(1, ch), lam.reshape(1, ch))


def _attn_prep_kernel(cq_ref, ckv_ref, kr_ref, cs_ref, qn_ref, kvn_ref, wq_ref, wkv_ref,
                      gq_ref, gk_ref, q_out, k_out, v_out, *, n_heads, q_scale):
    cq = cq_ref[...]
    ckv = ckv_ref[...]
    qa = jnp.dot((cq * _rms(cq) * qn_ref[...]).astype(BF16), wq_ref[...],
                 preferred_element_type=F32)
    kva = jnp.dot((ckv * _rms(ckv) * kvn_ref[...]).astype(BF16), wkv_ref[...],
                  preferred_element_type=F32)
    cs = cs_ref[...]
    lane = lax.broadcasted_iota(jnp.int32, cs.shape, 1)
    rope_lanes = lane < QK_ROPE
    gq = gq_ref[...]
    gk = gk_ref[...]
    vk = kr_ref[...]
    k_rope_ss = 0.5 * jnp.sum(vk * vk, axis=-1, keepdims=True)
    bk = vk * gk[:, LANES:] * cs
    bk = bk + pltpu.roll(bk, QK_ROPE, 1)
    ones = jnp.ones(cs.shape, BF16)
    for h in range(n_heads):
        base = h * HEAD_PAD
        qn = qa[:, base:base + LANES]
        qv = qa[:, base + LANES:base + HEAD_PAD]
        ss = jnp.sum(qn * qn, axis=-1, keepdims=True) + 0.5 * jnp.sum(qv * qv, axis=-1,
                                                                     keepdims=True)
        rs = lax.rsqrt(ss * (1.0 / QK_HEAD) + EPS) * q_scale
        bq = qv * gq[:, LANES:] * cs
        bq = bq + pltpu.roll(bq, QK_ROPE, 1)
        q_out[0, h, :, 0:LANES] = (qn * gq[:, :LANES] * rs).astype(BF16)
        q_out[0, h, :, LANES:] = (bq * rs).astype(BF16)

        kn = kva[:, base:base + LANES]
        ss = jnp.sum(kn * kn, axis=-1, keepdims=True) + k_rope_ss
        rs = lax.rsqrt(ss * (1.0 / QK_HEAD) + EPS)
        k_out[0, h, :, 0:LANES] = (kn * gk[:, :LANES] * rs).astype(BF16)
        k_out[0, h, :, LANES:] = jnp.where(rope_lanes, bk * rs, 0.0).astype(BF16)

        v_out[0, h, :, 0:LANES] = kva[:, base + LANES:base + HEAD_PAD].astype(BF16)
        v_out[0, h, :, LANES:] = ones


def attn_prep(h3, cs_table, q_norm, kv_norm, wq_all, wkv, gq, gk, tm, col_q, col_kv, col_kr):
    n_seq, s_len, _ = h3.shape
    q_lora = q_norm.shape[-1]
    kv_lora = kv_norm.shape[-1]
    n_heads = wq_all.shape[-1] // HEAD_PAD
    q_scale = QK_HEAD ** -0.5 * LOG2E
    out = jax.ShapeDtypeStruct((n_seq, n_heads, s_len, HEAD_PAD), BF16)
    head_spec = pl.BlockSpec((1, n_heads, tm, HEAD_PAD), lambda s, i: (s, 0, i, 0))
    const2 = lambda s, i: (0, 0)
    return pl.pallas_call(
        functools.partial(_attn_prep_kernel, n_heads=n_heads, q_scale=q_scale),
        out_shape=(out, out, out),
        grid=(n_seq, s_len // tm),
        in_specs=[
            pl.BlockSpec((None, tm, q_lora), lambda s, i: (s, i, col_q // q_lora)),
            pl.BlockSpec((None, tm, kv_lora), lambda s, i: (s, i, col_kv // kv_lora)),
            pl.BlockSpec((None, tm, LANES), lambda s, i: (s, i, col_kr // LANES)),
            pl.BlockSpec((tm, LANES), lambda s, i: (i, 0)),
            pl.BlockSpec((1, q_lora), const2),
            pl.BlockSpec((1, kv_lora), const2),
            pl.BlockSpec(wq_all.shape, const2),
            pl.BlockSpec(wkv.shape, const2),
            pl.BlockSpec((1, HEAD_PAD), const2),
            pl.BlockSpec((1, HEAD_PAD), const2),
        ],
        out_specs=(head_spec, head_spec, head_spec),
        compiler_params=_params("parallel", "parallel"),
        name="attn_prep",
    )(h3, h3, h3, cs_table, q_norm.reshape(1, -1), kv_norm.reshape(1, -1), wq_all, wkv, gq, gk)


def _flash_kernel(q_ref, k_ref, v_ref, o_ref, s_a, s_b, p_a, p_b, al_a, al_b, m_sc, acc_sc,
                  *, tk, n_kv):
    q = q_ref[0, 0]
    m_sc[...] = jnp.full(m_sc.shape, -jnp.inf, F32)
    acc_sc[...] = jnp.zeros(acc_sc.shape, F32)

    def scores(j, s_ref):
        k = k_ref[0, 0, j * tk:(j + 1) * tk, :]
        s_ref[...] = lax.dot_general(q, k, (((1,), (1,)), ((), ())),
                                     preferred_element_type=F32)

    def softmax(s_ref, p_ref, al_ref):
        s = s_ref[...]
        m_prev = m_sc[...]
        m_new = jnp.maximum(m_prev, jnp.max(s, axis=-1, keepdims=True))
        al_ref[...] = jnp.exp2(m_prev - m_new)
        p_ref[...] = jnp.exp2(s - m_new[:, :1]).astype(BF16)
        m_sc[...] = m_new

    def values(j, p_ref, al_ref):
        v = v_ref[0, 0, j * tk:(j + 1) * tk, :]
        pv = jnp.dot(p_ref[...], v, preferred_element_type=F32)
        alpha = al_ref[...]
        acc_sc[:, :LANES] = alpha * acc_sc[:, :LANES] + pv[:, :LANES]
        acc_sc[:, LANES:] = alpha * acc_sc[:, LANES:] + pv[:, LANES:]

    bufs = ((s_a, p_a, al_a), (s_b, p_b, al_b))
    scores(0, s_a)
    for j in range(n_kv):
        s_cur, p_cur, al_cur = bufs[j % 2]
        s_nxt, p_prev, al_prev = bufs[(j + 1) % 2]
        if j + 1 < n_kv:
            scores(j + 1, s_nxt)
        softmax(s_cur, p_cur, al_cur)
        if j >= 1:
            values(j - 1, p_prev, al_prev)
    _, p_last, al_last = bufs[(n_kv - 1) % 2]
    values(n_kv - 1, p_last, al_last)
    o_ref[0] = (acc_sc[:, :LANES] / acc_sc[:, LANES:]).astype(o_ref.dtype)


def flash_attention(q, k, v, tq, tk):
    n_seq, n_heads, s_len, dq = q.shape
    n_kv = s_len // tk
    kv_spec = pl.BlockSpec((1, 1, s_len, dq), lambda s, h, i: (s, h, 0, 0))
    return pl.pallas_call(
        functools.partial(_flash_kernel, tk=tk, n_kv=n_kv),
        out_shape=jax.ShapeDtypeStruct((n_seq, s_len, n_heads * V_HEAD_DIM), BF16),
        grid=(n_seq, n_heads, s_len // tq),
        in_specs=[pl.BlockSpec((1, 1, tq, dq), lambda s, h, i: (s, h, i, 0)), kv_spec, kv_spec],
        out_specs=pl.BlockSpec((1, tq, V_HEAD_DIM), lambda s, h, i: (s, i, h)),
        scratch_shapes=[pltpu.VMEM((tq, tk), F32), pltpu.VMEM((tq, tk), F32),
                        pltpu.VMEM((tq, tk), BF16), pltpu.VMEM((tq, tk), BF16),
                        pltpu.VMEM((tq, LANES), F32), pltpu.VMEM((tq, LANES), F32),
                        pltpu.VMEM((tq, LANES), F32), pltpu.VMEM((tq, dq), F32)],
        compiler_params=_params("parallel", "parallel", "arbitrary"),
        name="flash_attention",
    )(q, k, v)


def _mix_out_kernel(hf_ref, hb_ref, y_ref, att_ref, x_ref, w_ref, g_ref, wr_ref,
                    x1_ref, xn_ref, lg_ref):
    ch = hf_ref.shape[-1]
    y = y_ref[...]
    gelu = 0.5 * y * (1.0 + jnp.tanh(math.sqrt(2.0 / math.pi) * (y + 0.044715 * (y * y * y))))
    rnn = ((hf_ref[...] + hb_ref[...]) * gelu).astype(BF16)
    x1 = (x_ref[...]
          + jnp.dot(rnn, w_ref[0:ch, :], preferred_element_type=F32)
          + jnp.dot(att_ref[...], w_ref[ch:, :], preferred_element_type=F32))
    x1_ref[...] = x1
    xn = x1 * _rms(x1) * g_ref[...]
    xn_ref[...] = xn.astype(BF16)
    lg_ref[...] = lax.dot_general(wr_ref[...], xn, (((1,), (1,)), ((), ())),
                                  precision=lax.Precision.HIGHEST,
                                  preferred_element_type=F32)


def mix_out(hf, hb, h2, att, x, w_out, ln2, w_router_t, tm):
    rows, ch = hf.shape
    d = x.shape[-1]
    n_e = w_router_t.shape[0]
    row_spec = lambda width: pl.BlockSpec((tm, width), lambda i: (i, 0))
    return pl.pallas_call(
        _mix_out_kernel,
        out_shape=(jax.ShapeDtypeStruct((rows, d), F32),
                   jax.ShapeDtypeStruct((rows, d), BF16),
                   jax.ShapeDtypeStruct((n_e, rows), F32)),
        grid=(rows // tm,),
        in_specs=[row_spec(ch), row_spec(ch),
                  pl.BlockSpec((tm, ch), lambda i: (i, 1)),
                  row_spec(att.shape[-1]), row_spec(d),
                  pl.BlockSpec(w_out.shape, lambda i: (0, 0)),
                  pl.BlockSpec((1, d), lambda i: (0, 0)),
                  pl.BlockSpec((n_e, d), lambda i: (0, 0))],
        out_specs=(row_spec(d), row_spec(d), pl.BlockSpec((n_e, tm), lambda i: (0, i))),
        compiler_params=_params("parallel"),
        name="mix_out",
    )(hf, hb, h2, att, x, w_out, ln2.reshape(1, d), w_router_t)


def _ffn_kernel(xe_ref, gate_ref, w1_ref, w3_ref, w2_ref, o_ref):
    f = pl.program_id(1)
    x = xe_ref[0]
    a = jnp.dot(x, w1_ref[0], preferred_element_type=F32)
    b = jnp.dot(x, w3_ref[0], preferred_element_type=F32)
    hid = (a * jax.nn.sigmoid(a)) * b
    y = jnp.dot(hid.astype(BF16), w2_ref[0], preferred_element_type=F32)

    @pl.when(f == 0)
    def _():
        o_ref[0] = y

    @pl.when(f > 0)
    def _():
        o_ref[0] += y

    @pl.when(f == pl.num_programs(1) - 1)
    def _():
        o_ref[0] = o_ref[0] * gate_ref[0]


def expert_ffn(xe, gate, w1, w3, w2, tiles_per_expert, tf):
    n_tiles, tm, d = xe.shape
    d_ff = w1.shape[-1]
    e_of = lambda t: t // tiles_per_expert
    return pl.pallas_call(
        _ffn_kernel,
        out_shape=jax.ShapeDtypeStruct((n_tiles, tm, d), F32),
        grid=(n_tiles, d_ff // tf),
        in_specs=[pl.BlockSpec((1, tm, d), lambda t, f: (t, 0, 0)),
                  pl.BlockSpec((1, tm, 1), lambda t, f: (t, 0, 0)),
                  pl.BlockSpec((1, d, tf), lambda t, f: (e_of(t), 0, f)),
                  pl.BlockSpec((1, d, tf), lambda t, f: (e_of(t), 0, f)),
                  pl.BlockSpec((1, tf, d), lambda t, f: (e_of(t), f, 0))],
        out_specs=pl.BlockSpec((1, tm, d), lambda t, f: (t, 0, 0)),
        compiler_params=_params("parallel", "arbitrary"),
        name="expert_ffn",
    )(xe, gate, w1, w3, w2)


def _swap_halves(w):
    half = w.shape[-1] // 2
    return jnp.concatenate([w[..., half:], w[..., :half]], axis=-1)


def _prep_layer_params(p):
    d_lru = p["conv_w"].shape[-1]
    q_lora = p["q_norm"].shape[-1]
    kv_lora = p["kv_norm"].shape[-1]
    n_heads = p["w_q_up"].shape[-1] // QK_HEAD
    s3 = 2 * d_lru + q_lora + kv_lora
    w_in = p["w_in"]
    w_in_ext = jnp.concatenate([w_in, _swap_halves(w_in[:, s3:])], axis=-1).astype(BF16)
    wq = p["w_q_up"].reshape(q_lora, n_heads, QK_HEAD)
    wq_rope = wq[..., QK_NOPE:]
    wq_all = jnp.concatenate([wq[..., :QK_NOPE], wq_rope, _swap_halves(wq_rope)], axis=-1)
    wq_all = wq_all.reshape(q_lora, n_heads * HEAD_PAD).astype(BF16)

    def head_gain(g):
        g_rope = g[QK_NOPE:]
        return jnp.concatenate([g[:QK_NOPE], g_rope, _swap_halves(g_rope)]).reshape(1, HEAD_PAD)

    return dict(
        ln1=p["ln1"], w_in_ext=w_in_ext, conv_w=p["conv_w"], conv_b=p["conv_b"],
        w_gate=[jnp.concatenate([p["lru_w_a"][r], p["lru_w_x"][r]], axis=-1).astype(BF16)
                for r in range(2)],
        lru_b_a=p["lru_b_a"], lru_b_x=p["lru_b_x"], lru_lam=p["lru_lam"],
        q_norm=p["q_norm"], kv_norm=p["kv_norm"], wq_all=wq_all,
        wkv=p["w_kv_up"].astype(BF16), gq=head_gain(p["q_head_norm"]),
        gk=head_gain(p["k_head_norm"]), w_out=p["w_out"].astype(BF16), ln2=p["ln2"],
        w_router_t=p["w_router"].T, w1=p["w1"].astype(BF16), w3=p["w3"].astype(BF16),
        w2=p["w2"].astype(BF16), col_q=2 * d_lru, col_kv=2 * d_lru + q_lora, col_kr=s3)


def _rope_cs_table(s_len):
    pos = jnp.arange(s_len, dtype=F32)
    inv = ROPE_BASE ** (-jnp.arange(0, QK_ROPE, 2, dtype=F32) / QK_ROPE)
    ang = pos[:, None] * inv[None, :]
    cos, sin = jnp.cos(ang), jnp.sin(ang)
    return jnp.concatenate([cos, cos, -sin, sin], axis=-1)


def _layer(x, n_prompt_seq, p, cs_table):
    n_seq, s_len, d = x.shape
    rows = n_seq * s_len
    d_lru = p["conv_w"].shape[-1]

    h = norm_matmul(x.reshape(rows, d), p["ln1"], p["w_in_ext"], tm=_tile(rows, 256))
    h3 = h.reshape(n_seq, s_len, -1)

    t_len = _tile(s_len, 512)
    h_dir = [lru_scan(h3, p["conv_w"], p["conv_b"], p["w_gate"][r], p["lru_b_a"][r],
                      p["lru_b_x"][r], p["lru_lam"][r], t_len, reverse=bool(r))
             for r in range(2)]

    q, k, v = attn_prep(h3, cs_table, p["q_norm"], p["kv_norm"], p["wq_all"], p["wkv"],
                        p["gq"], p["gk"], _tile(s_len, 256), p["col_q"], p["col_kv"],
                        p["col_kr"])
    att = flash_attention(q, k, v, tq=_tile(s_len, 1024), tk=_tile(s_len, 1024))

    x1, xn_bf, logits_t = mix_out(
        h_dir[0].reshape(rows, d_lru), h_dir[1].reshape(rows, d_lru), h,
        att.reshape(rows, -1), x.reshape(rows, d), p["w_out"], p["ln2"], p["w_router_t"],
        tm=_tile(rows, 256))
    aff_t = jax.nn.softmax(logits_t, axis=0)

    outs = []
    for lo, hi in ((0, n_prompt_seq * s_len), (n_prompt_seq * s_len, rows)):
        n = hi - lo
        cap = max(1, (CAPACITY * n) // N_EXPERTS)
        gate, idx = lax.top_k(aff_t[:, lo:hi], cap)
        xe = xn_bf[lo:hi][idx]
        tm = _tile(cap, 1024)
        tiles_per_expert = cap // tm
        ye = expert_ffn(xe.reshape(N_EXPERTS * tiles_per_expert, tm, d),
                        gate.reshape(N_EXPERTS * tiles_per_expert, tm, 1),
                        p["w1"], p["w3"], p["w2"], tiles_per_expert, tf=_tile(d, 512))
        out = jnp.zeros((n, d), F32).at[idx.reshape(-1)].add(ye.reshape(-1, d))
        outs.append(x1[lo:hi] + out)
    return jnp.concatenate(outs, axis=0).reshape(n_seq, s_len, d)


def kernel(x_prompt, x_sample, ln1, w_in, conv_w, conv_b, lru_w_a, lru_b_a, lru_w_x, lru_b_x,
           lru_lam, q_norm, w_q_up, kv_norm, w_kv_up, q_head_norm, k_head_norm, w_out, ln2,
           w_router, w1, w3, w2, final_norm):
    depth = ln1.shape[0]
    n_prompt_seq = x_prompt.shape[0]
    x = jnp.concatenate([x_prompt, x_sample], axis=0)
    n_seq, s_len, d = x.shape
    cs_table = _rope_cs_table(s_len)
    for l in range(depth):
        p = _prep_layer_params(dict(
            ln1=ln1[l], w_in=w_in[l], conv_w=conv_w[l], conv_b=conv_b[l], lru_w_a=lru_w_a[l],
            lru_b_a=lru_b_a[l], lru_w_x=lru_w_x[l], lru_b_x=lru_b_x[l], lru_lam=lru_lam[l],
            q_norm=q_norm[l], w_q_up=w_q_up[l], kv_norm=kv_norm[l], w_kv_up=w_kv_up[l],
            q_head_norm=q_head_norm[l], k_head_norm=k_head_norm[l], w_out=w_out[l], ln2=ln2[l],
            w_router=w_router[l], w1=w1[l], w3=w3[l], w2=w2[l]))
        x = _layer(x, n_prompt_seq, p, cs_table)
    y = rms_norm_rows(x.reshape(n_seq * s_len, d), final_norm, _tile(n_seq * s_len, 512))
    y = y.reshape(n_seq, s_len, d)
    return y[:n_prompt_seq], y[n_prompt_seq:]
```

```python
import functools
import math

import jax
import jax.numpy as jnp
from jax import lax
from jax.experimental import pallas as pl
from jax.experimental.pallas import tpu as pltpu

F32 = jnp.float32
BF16 = jnp.bfloat16

EPS = 1e-6
LRU_BLOCKS = 8
CONV_WIDTH = 4
LRU_C = 8.0
V_HEAD_DIM = 128
QK_NOPE = 128
QK_ROPE = 64
QK_HEAD = QK_NOPE + QK_ROPE
ROPE_BASE = 10000.0
N_EXPERTS = 16
CAPACITY = 2

LANES = 128
SUBLANES = 8
HEAD_PAD = 2 * LANES
V7X_VMEM_LIMIT_BYTES = 56 * 1024 * 1024
LOG2E = math.log2(math.e)


def _tile(n, pref):
    t = min(n, pref)
    while n % t:
        t //= 2
    return t


def _params(*semantics):
    return pltpu.CompilerParams(dimension_semantics=semantics,
                                vmem_limit_bytes=V7X_VMEM_LIMIT_BYTES)


def _rms(x):
    return lax.rsqrt(jnp.mean(x * x, axis=-1, keepdims=True) + EPS)


def _store_row_linear(ref, base, x):
    n_rows, d = x.shape
    pitch = d // LANES
    for c in range(pitch):
        ref[pl.ds(base + c, n_rows, stride=pitch), :] = x[:, c * LANES:(c + 1) * LANES]


def _load_row_linear(ref, base, n_rows, pitch):
    return jnp.concatenate(
        [ref[pl.ds(base + c, n_rows, stride=pitch), :] for c in range(pitch)], axis=1)


def _norm_matmul_kernel(x_ref, g_ref, w_ref, o_ref):
    x = x_ref[...]
    y = x * _rms(x) * g_ref[...]
    o_ref[...] = jnp.dot(y.astype(BF16), w_ref[...], preferred_element_type=F32)


def norm_matmul(x, g, w_bf16, tm):
    rows, d = x.shape
    n = w_bf16.shape[1]
    return pl.pallas_call(
        _norm_matmul_kernel,
        out_shape=jax.ShapeDtypeStruct((rows, n), F32),
        grid=(rows // tm,),
        in_specs=[pl.BlockSpec((tm, d), lambda i: (i, 0)),
                  pl.BlockSpec((1, d), lambda i: (0, 0)),
                  pl.BlockSpec((d, n), lambda i: (0, 0))],
        out_specs=pl.BlockSpec((tm, n), lambda i: (i, 0)),
        compiler_params=_params("parallel"),
        name="norm_matmul",
    )(x, g.reshape(1, d), w_bf16)


def _norm_kernel(x_ref, g_ref, o_ref):
    x = x_ref[...]
    o_ref[...] = x * _rms(x) * g_ref[...]


def rms_norm_rows(x, g, tm):
    rows, d = x.shape
    return pl.pallas_call(
        _norm_kernel,
        out_shape=jax.ShapeDtypeStruct((rows, d), F32),
        grid=(rows // tm,),
        in_specs=[pl.BlockSpec((tm, d), lambda i: (i, 0)),
                  pl.BlockSpec((1, d), lambda i: (0, 0))],
        out_specs=pl.BlockSpec((tm, d), lambda i: (i, 0)),
        compiler_params=_params("parallel"),
        name="final_norm",
    )(x, g.reshape(1, d))


def _lru_kernel(x_ref, prev_ref, next_ref, cw_ref, cb_ref, wg_ref, ba_ref, bx_ref, lam_ref,
                o_ref, ext_sc, a_sc, u_sc, carry_sc, *, reverse):
    c = pl.program_id(1)
    n_c = pl.num_programs(1)
    t_len, ch = a_sc.shape
    blk = ch // LRU_BLOCKS
    first_in_time = (c == n_c - 1) if reverse else (c == 0)
    last_in_time = (c == 0) if reverse else (c == n_c - 1)

    @pl.when(c == 0)
    def _():
        carry_sc[...] = jnp.zeros(carry_sc.shape, F32)

    ext_sc[0:SUBLANES, :] = jnp.where(first_in_time, 0.0, prev_ref[0])
    ext_sc[SUBLANES:SUBLANES + t_len, :] = x_ref[0]
    ext_sc[SUBLANES + t_len:, :] = jnp.where(last_in_time, 0.0, next_ref[0])
    xc = cb_ref[...]
    for tap in range(CONV_WIDTH):
        off = SUBLANES + tap - 2
        xc = xc + cw_ref[tap:tap + 1, :] * ext_sc[off:off + t_len, :]

    lam = lam_ref[...]
    log_sig_lam = jnp.minimum(lam, 0.0) - jnp.log1p(jnp.exp(-jnp.abs(lam)))
    for n in range(LRU_BLOCKS):
        sl = slice(n * blk, (n + 1) * blk)
        xcn = xc[:, sl]
        g = jnp.dot(xcn.astype(BF16), wg_ref[n], preferred_element_type=F32)
        r = jax.nn.sigmoid(g[:, :blk] + ba_ref[:, sl])
        i = jax.nn.sigmoid(g[:, blk:] + bx_ref[:, sl])
        log_a = LRU_C * r * log_sig_lam[:, sl]
        a_sc[:, sl] = jnp.exp(log_a)
        u_sc[:, sl] = jnp.sqrt(1.0 - jnp.exp(2.0 * log_a)) * (i * xcn)

    row = lax.broadcasted_iota(jnp.int32, (SUBLANES, ch), 0)
    n_groups = t_len // SUBLANES

    def group(gi, carry):
        g0 = (n_groups - 1 - gi) if reverse else gi
        start = pl.multiple_of(g0 * SUBLANES, SUBLANES)
        a = a_sc[pl.ds(start, SUBLANES), :]
        u = u_sc[pl.ds(start, SUBLANES), :]
        for d in (1, 2, 4):
            shift = SUBLANES - d if reverse else d
            a_s = pltpu.roll(a, shift, 0)
            u_s = pltpu.roll(u, shift, 0)
            valid = (row < SUBLANES - d) if reverse else (row >= d)
            u = jnp.where(valid, a * u_s + u, u)
            a = jnp.where(valid, a * a_s, a)
        h = u + a * carry
        o_ref[0, pl.ds(start, SUBLANES), :] = h
        edge = h[0:1, :] if reverse else h[SUBLANES - 1:SUBLANES, :]
        return jnp.broadcast_to(edge, (SUBLANES, ch))

    carry_sc[...] = lax.fori_loop(0, n_groups, group, carry_sc[...], unroll=2)


def lru_scan(h3, conv_w, conv_b, w_gate, b_a, b_x, lam, t_len, reverse):
    n_seq, s_len, _ = h3.shape
    ch = conv_w.shape[-1]
    n_c = s_len // t_len
    per = t_len // SUBLANES
    n_sub = s_len // SUBLANES
    cidx = (lambda c: n_c - 1 - c) if reverse else (lambda c: c)
    return pl.pallas_call(
        functools.partial(_lru_kernel, reverse=reverse),
        out_shape=jax.ShapeDtypeStruct((n_seq, s_len, ch), F32),
        grid=(n_seq, n_c),
        in_specs=[
            pl.BlockSpec((1, t_len, ch), lambda s, c: (s, cidx(c), 0)),
            pl.BlockSpec((1, SUBLANES, ch),
                         lambda s, c: (s, jnp.maximum(cidx(c) * per - 1, 0), 0)),
            pl.BlockSpec((1, SUBLANES, ch),
                         lambda s, c: (s, jnp.minimum((cidx(c) + 1) * per, n_sub - 1), 0)),
            pl.BlockSpec((CONV_WIDTH, ch), lambda s, c: (0, 0)),
            pl.BlockSpec((1, ch), lambda s, c: (0, 0)),
            pl.BlockSpec(w_gate.shape, lambda s, c: (0, 0, 0)),
            pl.BlockSpec((1, ch), lambda s, c: (0, 0)),
            pl.BlockSpec((1, ch), lambda s, c: (0, 0)),
            pl.BlockSpec((1, ch), lambda s, c: (0, 0)),
        ],
        out_specs=pl.BlockSpec((1, t_len, ch), lambda s, c: (s, cidx(c), 0)),
        scratch_shapes=[pltpu.VMEM((t_len + 2 * SUBLANES, ch), F32),
                        pltpu.VMEM((t_len, ch), F32),
                        pltpu.VMEM((t_len, ch), F32),
                        pltpu.VMEM((SUBLANES, ch), F32)],
        compiler_params=_params("parallel", "arbitrary"),
        name="lru_rev" if reverse else "lru_fwd",
    )(h3, h3, h3, conv_w, conv_b.reshape(1, ch), w_gate, b_a.reshape(1, ch),
      b_x.reshape(1, ch), lam.reshape(1, ch))


def _attn_prep_kernel(cq_ref, ckv_ref, kr_ref, cs_ref, qn_ref, kvn_ref, wq_ref, wkv_ref,
                      gq_ref, gk_ref, q_out, k_out, v_out, *, n_heads, q_scale):
    cq = cq_ref[...]
    ckv = ckv_ref[...]
    qa = jnp.dot((cq * _rms(cq) * qn_ref[...]).astype(BF16), wq_ref[...],
                 preferred_element_type=F32)
    kva = jnp.dot((ckv * _rms(ckv) * kvn_ref[...]).astype(BF16), wkv_ref[...],
                  preferred_element_type=F32)
    cs = cs_ref[...]
    lane = lax.broadcasted_iota(jnp.int32, cs.shape, 1)
    rope_lanes = lane < QK_ROPE
    gq = gq_ref[...]
    gk = gk_ref[...]
    vk = kr_ref[...]
    k_rope_ss = 0.5 * jnp.sum(vk * vk, axis=-1, keepdims=True)
    bk = vk * gk[:, LANES:] * cs
    bk = bk + pltpu.roll(bk, QK_ROPE, 1)
    ones = jnp.ones(cs.shape, BF16)
    for h in range(n_heads):
        base = h * HEAD_PAD
        qn = qa[:, base:base + LANES]
        qv = qa[:, base + LANES:base + HEAD_PAD]
        ss = jnp.sum(qn * qn, axis=-1, keepdims=True) + 0.5 * jnp.sum(qv * qv, axis=-1,
                                                                     keepdims=True)
        rs = lax.rsqrt(ss * (1.0 / QK_HEAD) + EPS) * q_scale
        bq = qv * gq[:, LANES:] * cs
        bq = bq + pltpu.roll(bq, QK_ROPE, 1)
        q_out[0, h, :, 0:LANES] = (qn * gq[:, :LANES] * rs).astype(BF16)
        q_out[0, h, :, LANES:] = (bq * rs).astype(BF16)

        kn = kva[:, base:base + LANES]
        ss = jnp.sum(kn * kn, axis=-1, keepdims=True) + k_rope_ss
        rs = lax.rsqrt(ss * (1.0 / QK_HEAD) + EPS)
        k_out[0, h, :, 0:LANES] = (kn * gk[:, :LANES] * rs).astype(BF16)
        k_out[0, h, :, LANES:] = jnp.where(rope_lanes, bk * rs, 0.0).astype(BF16)

        v_out[0, h, :, 0:LANES] = kva[:, base + LANES:base + HEAD_PAD].astype(BF16)
        v_out[0, h, :, LANES:] = ones


def attn_prep(h3, cs_table, q_norm, kv_norm, wq_all, wkv, gq, gk, tm, col_q, col_kv, col_kr):
    n_seq, s_len, _ = h3.shape
    q_lora = q_norm.shape[-1]
    kv_lora = kv_norm.shape[-1]
    n_heads = wq_all.shape[-1] // HEAD_PAD
    q_scale = QK_HEAD ** -0.5 * LOG2E
    out = jax.ShapeDtypeStruct((n_seq, n_heads, s_len, HEAD_PAD), BF16)
    head_spec = pl.BlockSpec((1, n_heads, tm, HEAD_PAD), lambda s, i: (s, 0, i, 0))
    const2 = lambda s, i: (0, 0)
    return pl.pallas_call(
        functools.partial(_attn_prep_kernel, n_heads=n_heads, q_scale=q_scale),
        out_shape=(out, out, out),
        grid=(n_seq, s_len // tm),
        in_specs=[
            pl.BlockSpec((None, tm, q_lora), lambda s, i: (s, i, col_q // q_lora)),
            pl.BlockSpec((None, tm, kv_lora), lambda s, i: (s, i, col_kv // kv_lora)),
            pl.BlockSpec((None, tm, LANES), lambda s, i: (s, i, col_kr // LANES)),
            pl.BlockSpec((tm, LANES), lambda s, i: (i, 0)),
            pl.BlockSpec((1, q_lora), const2),
            pl.BlockSpec((1, kv_lora), const2),
            pl.BlockSpec(wq_all.shape, const2),
            pl.BlockSpec(wkv.shape, const2),
            pl.BlockSpec((1, HEAD_PAD), const2),
            pl.BlockSpec((1, HEAD_PAD), const2),
        ],
        out_specs=(head_spec, head_spec, head_spec),
        compiler_params=_params("parallel", "parallel"),
        name="attn_prep",
    )(h3, h3, h3, cs_table, q_norm.reshape(1, -1), kv_norm.reshape(1, -1), wq_all, wkv, gq, gk)


def _flash_kernel(q_ref, k_ref, v_ref, o_ref, s_a, s_b, p_a, p_b, al_a, al_b, m_sc, acc_sc,
                  *, tk, n_kv):
    q = q_ref[0, 0]
    m_sc[...] = jnp.full(m_sc.shape, -jnp.inf, F32)
    acc_sc[...] = jnp.zeros(acc_sc.shape, F32)

    def scores(j, s_ref):
        k = k_ref[0, 0, j * tk:(j + 1) * tk, :]
        s_ref[...] = lax.dot_general(q, k, (((1,), (1,)), ((), ())),
                                     preferred_element_type=F32)

    def softmax(s_ref, p_ref, al_ref):
        s = s_ref[...]
        m_prev = m_sc[...]
        m_new = jnp.maximum(m_prev, jnp.max(s, axis=-1, keepdims=True))
        al_ref[...] = jnp.exp2(m_prev - m_new)
        p_ref[...] = jnp.exp2(s - m_new[:, :1]).astype(BF16)
        m_sc[...] = m_new

    def values(j, p_ref, al_ref):
        v = v_ref[0, 0, j * tk:(j + 1) * tk, :]
        pv = jnp.dot(p_ref[...], v, preferred_element_type=F32)
        alpha = al_ref[...]
        acc_sc[:, :LANES] = alpha * acc_sc[:, :LANES] + pv[:, :LANES]
        acc_sc[:, LANES:] = alpha * acc_sc[:, LANES:] + pv[:, LANES:]

    bufs = ((s_a, p_a, al_a), (s_b, p_b, al_b))
    scores(0, s_a)
    for j in range(n_kv):
        s_cur, p_cur, al_cur = bufs[j % 2]
        s_nxt, p_prev, al_prev = bufs[(j + 1) % 2]
        if j + 1 < n_kv:
            scores(j + 1, s_nxt)
        softmax(s_cur, p_cur, al_cur)
        if j >= 1:
            values(j - 1, p_prev, al_prev)
    _, p_last, al_last = bufs[(n_kv - 1) % 2]
    values(n_kv - 1, p_last, al_last)
    o_ref[0] = (acc_sc[:, :LANES] / acc_sc[:, LANES:]).astype(o_ref.dtype)


def flash_attention(q, k, v, tq, tk):
    n_seq, n_heads, s_len, dq = q.shape
    n_kv = s_len // tk
    kv_spec = pl.BlockSpec((1, 1, s_len, dq), lambda s, h, i: (s, h, 0, 0))
    return pl.pallas_call(
        functools.partial(_flash_kernel, tk=tk, n_kv=n_kv),
        out_shape=jax.ShapeDtypeStruct((n_seq, s_len, n_heads * V_HEAD_DIM), BF16),
        grid=(n_seq, n_heads, s_len // tq),
        in_specs=[pl.BlockSpec((1, 1, tq, dq), lambda s, h, i: (s, h, i, 0)), kv_spec, kv_spec],
        out_specs=pl.BlockSpec((1, tq, V_HEAD_DIM), lambda s, h, i: (s, i, h)),
        scratch_shapes=[pltpu.VMEM((tq, tk), F32), pltpu.VMEM((tq, tk), F32),
                        pltpu.VMEM((tq, tk), BF16), pltpu.VMEM((tq, tk), BF16),
                        pltpu.VMEM((tq, LANES), F32), pltpu.VMEM((tq, LANES), F32),
                        pltpu.VMEM((tq, LANES), F32), pltpu.VMEM((tq, dq), F32)],
        compiler_params=_params("parallel", "parallel", "arbitrary"),
        name="flash_attention",
    )(q, k, v)


def _mix_out_kernel(hf_ref, hb_ref, y_ref, att_ref, x_ref, w_ref, g_ref, wr_ref,
                    x1_ref, xn_ref, lg_ref):
    ch = hf_ref.shape[-1]
    y = y_ref[...]
    gelu = 0.5 * y * (1.0 + jnp.tanh(math.sqrt(2.0 / math.pi) * (y + 0.044715 * (y * y * y))))
    rnn = ((hf_ref[...] + hb_ref[...]) * gelu).astype(BF16)
    x1 = (x_ref[...]
          + jnp.dot(rnn, w_ref[0:ch, :], preferred_element_type=F32)
          + jnp.dot(att_ref[...], w_ref[ch:, :], preferred_element_type=F32))
    xn = x1 * _rms(x1) * g_ref[...]
    _store_row_linear(x1_ref, 0, x1)
    _store_row_linear(xn_ref, 0, xn)
    lg_ref[...] = lax.dot_general(wr_ref[...], xn, (((1,), (1,)), ((), ())),
                                  precision=lax.Precision.HIGHEST,
                                  preferred_element_type=F32)


def mix_out(hf, hb, h2, att, x, w_out, ln2, w_router_t, tm):
    rows, ch = hf.shape
    d = x.shape[-1]
    pitch = d // LANES
    n_e = w_router_t.shape[0]
    row_spec = lambda width: pl.BlockSpec((tm, width), lambda i: (i, 0))
    lin_spec = pl.BlockSpec((tm * pitch, LANES), lambda i: (i, 0))
    lin_shape = jax.ShapeDtypeStruct((rows * pitch, LANES), F32)
    return pl.pallas_call(
        _mix_out_kernel,
        out_shape=(lin_shape, lin_shape, jax.ShapeDtypeStruct((n_e, rows), F32)),
        grid=(rows // tm,),
        in_specs=[row_spec(ch), row_spec(ch),
                  pl.BlockSpec((tm, ch), lambda i: (i, 1)),
                  row_spec(att.shape[-1]), row_spec(d),
                  pl.BlockSpec(w_out.shape, lambda i: (0, 0)),
                  pl.BlockSpec((1, d), lambda i: (0, 0)),
                  pl.BlockSpec((n_e, d), lambda i: (0, 0))],
        out_specs=(lin_spec, lin_spec, pl.BlockSpec((n_e, tm), lambda i: (0, i))),
        compiler_params=_params("parallel"),
        name="mix_out",
    )(hf, hb, h2, att, x, w_out, ln2.reshape(1, d), w_router_t)


def _sum_all(x):
    return jnp.sum(jnp.sum(x, axis=0, keepdims=True), axis=1, keepdims=True)


def _route_kernel(lg_ref, idx_ref, gate_ref, src_ref, cnt_ref, *, cap, row_offset,
                  slot_offset, s_chunk):
    e = pl.program_id(0)
    n_e, nb, _ = lg_ref.shape
    lg = lg_ref[...]
    m = jnp.max(lg, axis=0)
    den = jnp.sum(jnp.exp(lg - m[None]), axis=0)
    aff = jnp.exp(lg_ref[e] - m) / den
    bits = pltpu.bitcast(aff, jnp.int32)

    def bit_step(i, prefix):
        cand = prefix | jnp.left_shift(jnp.int32(1), 30 - i)
        n_ge = _sum_all(jnp.where(bits >= cand, 1.0, 0.0))
        return jnp.where(n_ge >= cap, cand, prefix)

    thr = lax.fori_loop(0, 31, bit_step, jnp.zeros((1, 1), jnp.int32))

    r128 = lax.broadcasted_iota(jnp.int32, (LANES, LANES), 0)
    c128 = lax.broadcasted_iota(jnp.int32, (LANES, LANES), 1)
    upper_lanes = jnp.where(r128 <= c128, 1.0, 0.0).astype(BF16)
    rb = lax.broadcasted_iota(jnp.int32, (nb, nb), 0)
    cb = lax.broadcasted_iota(jnp.int32, (nb, nb), 1)
    lower_rows = jnp.where(cb < rb, 1.0, 0.0).astype(BF16)
    upper_rows = jnp.where(rb <= cb, 1.0, 0.0).astype(BF16)

    def cumsum_tokens(mf):
        row_cum = jnp.dot(mf.astype(BF16), upper_lanes, preferred_element_type=F32)
        tot = jnp.broadcast_to(row_cum[:, LANES - 1:LANES], (nb, LANES))
        return row_cum + jnp.dot(lower_rows, tot.astype(BF16), preferred_element_type=F32)

    gt = bits > thr
    eq = bits == thr
    eq_f = jnp.where(eq, 1.0, 0.0)
    need = cap - _sum_all(jnp.where(gt, 1.0, 0.0))
    eq_rank = cumsum_tokens(eq_f) - eq_f
    sel = gt | (eq & (eq_rank < need))
    sel_f = jnp.where(sel, 1.0, 0.0)
    cum = cumsum_tokens(sel_f)

    @pl.when(e == 0)
    def _():
        cnt_ref[...] = jnp.zeros(cnt_ref.shape, jnp.int32)
        src_ref[...] = jnp.zeros(src_ref.shape, jnp.int32)

    rank = cnt_ref[...]
    flat = (cum - sel_f).astype(jnp.int32) + (slot_offset + e * cap)
    for k in range(n_e):
        src_ref[k] = jnp.where(sel & (rank == k), flat, src_ref[k])
    cnt_ref[...] = rank + sel.astype(jnp.int32)

    tot_lane = lax.dot_general(jnp.ones((SUBLANES, LANES), BF16), sel_f.astype(BF16),
                               (((1,), (1,)), ((), ())), preferred_element_type=F32)
    end_incl = jnp.dot(tot_lane.astype(BF16), upper_rows, preferred_element_type=F32)[0:1]
    end_excl = end_incl - tot_lane[0:1]
    cum_hi = jnp.floor(cum * (1.0 / 64.0))
    a1 = aff.astype(BF16)
    r1 = aff - a1.astype(F32)
    a2 = r1.astype(BF16)
    a3 = (r1 - a2.astype(F32)).astype(BF16)
    payload = jnp.concatenate(
        [cum_hi.astype(BF16), (cum - 64.0 * cum_hi).astype(BF16), a1, a2, a3], axis=1)
    row_id = lax.broadcasted_iota(jnp.int32, (s_chunk, nb), 1).astype(F32)
    lane_id = lax.broadcasted_iota(jnp.int32, (s_chunk, LANES), 1)

    def chunk(c, carry):
        base = pl.multiple_of(c * s_chunk, s_chunk)
        s_col = (lax.broadcasted_iota(jnp.int32, (s_chunk, 1), 0) + base).astype(F32)
        hit = (end_excl <= s_col) & (s_col < end_incl)
        hit_f = jnp.where(hit, 1.0, 0.0)
        got = jnp.dot(hit_f.astype(BF16), payload, preferred_element_type=F32)
        cum_row = 64.0 * got[:, 0:LANES] + got[:, LANES:2 * LANES]
        aff_row = got[:, 2 * LANES:3 * LANES] + got[:, 3 * LANES:4 * LANES] + got[:, 4 * LANES:]
        j = jnp.sum(jnp.where(cum_row <= s_col, 1.0, 0.0), axis=1, keepdims=True)
        blk = jnp.sum(hit_f * row_id, axis=1, keepdims=True)
        idx_ref[0, pl.ds(base, s_chunk), :] = (blk * LANES + j).astype(jnp.int32) + row_offset
        gate_ref[0, pl.ds(base, s_chunk), :] = jnp.sum(
            jnp.where(lane_id == j.astype(jnp.int32), aff_row, 0.0), axis=1, keepdims=True)
        return carry

    lax.fori_loop(0, cap // s_chunk, chunk, 0)


def route(lg, cap, row_offset, slot_offset):
    n_e, nb, _ = lg.shape
    assert cap <= 64 * 256, "slot counts are split into two bf16-exact digits"
    s_chunk = _tile(cap, 512)
    return pl.pallas_call(
        functools.partial(_route_kernel, cap=cap, row_offset=row_offset,
                          slot_offset=slot_offset, s_chunk=s_chunk),
        out_shape=(jax.ShapeDtypeStruct((n_e, cap, 1), jnp.int32),
                   jax.ShapeDtypeStruct((n_e, cap, 1), F32),
                   jax.ShapeDtypeStruct((n_e, nb, LANES), jnp.int32),
                   jax.ShapeDtypeStruct((nb, LANES), jnp.int32)),
        grid=(n_e,),
        in_specs=[pl.BlockSpec((n_e, nb, LANES), lambda e: (0, 0, 0))],
        out_specs=(pl.BlockSpec((1, cap, 1), lambda e: (e, 0, 0)),
                   pl.BlockSpec((1, cap, 1), lambda e: (e, 0, 0)),
                   pl.BlockSpec((n_e, nb, LANES), lambda e: (0, 0, 0)),
                   pl.BlockSpec((nb, LANES), lambda e: (0, 0))),
        compiler_params=_params("arbitrary"),
        name="route",
    )(lg)


def _ffn_gather_kernel(idx_ref, idx_next_ref, gate_ref, xn_hbm, w1_ref, w3_ref, w2_ref,
                       o_ref, xbuf, xs, acc, sem):
    t = pl.program_id(0)
    f = pl.program_id(1)
    n_t = pl.num_programs(0)
    tm, d = xs.shape
    pitch = d // LANES
    slot = t % 2

    def row_copy(idx_smem, r, slot_):
        src = pl.multiple_of(idx_smem[0, 0, r] * pitch, pitch)
        dst = pl.multiple_of((slot_ * tm + r) * pitch, pitch)
        return pltpu.make_async_copy(xn_hbm.at[pl.ds(src, pitch), :],
                                     xbuf.at[pl.ds(dst, pitch), :], sem.at[slot_])

    def start_rows(idx_smem, slot_):
        def body(r, carry):
            row_copy(idx_smem, r, slot_).start()
            return carry
        lax.fori_loop(0, tm, body, 0, unroll=8)

    def wait_rows(idx_smem, slot_):
        def body(r, carry):
            row_copy(idx_smem, r, slot_).wait()
            return carry
        lax.fori_loop(0, tm, body, 0, unroll=8)

    @pl.when(f == 0)
    def _():
        @pl.when(t == 0)
        def _():
            start_rows(idx_ref, 0)

        @pl.when(t + 1 < n_t)
        def _():
            start_rows(idx_next_ref, 1 - slot)

        wait_rows(idx_ref, slot)
        base = pl.multiple_of(slot * (tm * pitch), SUBLANES)
        xs[...] = _load_row_linear(xbuf, base, tm, pitch).astype(BF16)

    x = xs[...]
    a = jnp.dot(x, w1_ref[0], preferred_element_type=F32)
    b = jnp.dot(x, w3_ref[0], preferred_element_type=F32)
    hid = (a * jax.nn.sigmoid(a)) * b
    y = jnp.dot(hid.astype(BF16), w2_ref[0], preferred_element_type=F32)

    @pl.when(f == 0)
    def _():
        acc[...] = y

    @pl.when(f > 0)
    def _():
        acc[...] += y

    @pl.when(f == pl.num_programs(1) - 1)
    def _():
        _store_row_linear(o_ref, 0, acc[...] * gate_ref[0])


def expert_ffn_gather(idx_tiles, gate_tiles, xn_lin, d, w1, w3, w2, tile_expert, tf):
    n_tiles, _, tm = idx_tiles.shape
    pitch = d // LANES
    d_ff = w1.shape[-1]
    smem = pltpu.MemorySpace.SMEM
    return pl.pallas_call(
        _ffn_gather_kernel,
        out_shape=jax.ShapeDtypeStruct((n_tiles * tm * pitch, LANES), F32),
        grid=(n_tiles, d_ff // tf),
        in_specs=[pl.BlockSpec((1, 1, tm), lambda t, f: (t, 0, 0), memory_space=smem),
                  pl.BlockSpec((1, 1, tm), lambda t, f: (jnp.minimum(t + 1, n_tiles - 1), 0, 0),
                               memory_space=smem),
                  pl.BlockSpec((1, tm, 1), lambda t, f: (t, 0, 0)),
                  pl.BlockSpec(memory_space=pl.ANY),
                  pl.BlockSpec((1, d, tf), lambda t, f: (tile_expert(t), 0, f)),
                  pl.BlockSpec((1, d, tf), lambda t, f: (tile_expert(t), 0, f)),
                  pl.BlockSpec((1, tf, d), lambda t, f: (tile_expert(t), f, 0))],
        out_specs=pl.BlockSpec((tm * pitch, LANES), lambda t, f: (t, 0)),
        scratch_shapes=[pltpu.VMEM((2 * tm * pitch, LANES), F32),
                        pltpu.VMEM((tm, d), BF16),
                        pltpu.VMEM((tm, d), F32),
                        pltpu.SemaphoreType.DMA((2,))],
        compiler_params=_params("arbitrary", "arbitrary"),
        name="expert_ffn",
    )(idx_tiles, idx_tiles, gate_tiles, xn_lin, w1, w3, w2)


def _combine_kernel(cnt_ref, src_ref, cnt_next_ref, src_next_ref, x1_ref, ye_hbm, o_ref,
                    buf, acc, sem, meta):
    i = pl.program_id(0)
    n_i = pl.num_programs(0)
    slot = i % 2
    n_tok, d = o_ref.shape
    pitch = d // LANES
    n_e = src_ref.shape[1]
    slab = n_tok * pitch

    def row_copy(row, k, t, slot_):
        src = pl.multiple_of(row * pitch, pitch)
        dst = pl.multiple_of(((slot_ * n_e + k) * n_tok + t) * pitch, pitch)
        return pltpu.make_async_copy(ye_hbm.at[pl.ds(src, pitch), :],
                                     buf.at[pl.ds(dst, pitch), :], sem.at[slot_])

    def slab_start(k, slot_):
        return pl.multiple_of((slot_ * n_e + k) * slab, SUBLANES)

    def fetch(cnt_smem, src_smem, slot_):
        k_max = lax.fori_loop(0, n_tok, lambda t, m: jnp.maximum(m, cnt_smem[0, 0, t]), 0)

        def clear(k, carry):
            buf[pl.ds(slab_start(k, slot_), slab), :] = jnp.zeros((slab, LANES), F32)
            return carry
        lax.fori_loop(0, k_max, clear, 0)

        def token(t, total):
            c = cnt_smem[0, 0, t]

            def one(k, carry):
                row_copy(src_smem[0, k, t], k, t, slot_).start()
                return carry
            lax.fori_loop(0, c, one, 0)
            return total + c
        meta[slot_, 0] = lax.fori_loop(0, n_tok, token, 0)
        meta[slot_, 1] = k_max

    @pl.when(i == 0)
    def _():
        fetch(cnt_ref, src_ref, 0)

    @pl.when(i + 1 < n_i)
    def _():
        fetch(cnt_next_ref, src_next_ref, 1 - slot)

    def wait_one(r, carry):
        row_copy(0, 0, 0, slot).wait()
        return carry
    lax.fori_loop(0, meta[slot, 0], wait_one, 0)

    acc[...] = x1_ref[...]

    def add(k, carry):
        acc[...] += buf[pl.ds(slab_start(k, slot), slab), :]
        return carry
    lax.fori_loop(0, meta[slot, 1], add, 0)
    o_ref[...] = _load_row_linear(acc, 0, n_tok, pitch)


def combine(cnt_blocks, src_blocks, x1_lin, ye_lin, d):
    n_blk, n_e, n_tok = src_blocks.shape
    pitch = d // LANES
    smem = pltpu.MemorySpace.SMEM
    nxt = lambda i: jnp.minimum(i + 1, n_blk - 1)
    return pl.pallas_call(
        _combine_kernel,
        out_shape=jax.ShapeDtypeStruct((n_blk * n_tok, d), F32),
        grid=(n_blk,),
        in_specs=[pl.BlockSpec((1, 1, n_tok), lambda i: (i, 0, 0), memory_space=smem),
                  pl.BlockSpec((1, n_e, n_tok), lambda i: (i, 0, 0), memory_space=smem),
                  pl.BlockSpec((1, 1, n_tok), lambda i: (nxt(i), 0, 0), memory_space=smem),
                  pl.BlockSpec((1, n_e, n_tok), lambda i: (nxt(i), 0, 0), memory_space=smem),
                  pl.BlockSpec((n_tok * pitch, LANES), lambda i: (i, 0)),
                  pl.BlockSpec(memory_space=pl.ANY)],
        out_specs=pl.BlockSpec((n_tok, d), lambda i: (i, 0)),
        scratch_shapes=[pltpu.VMEM((2 * n_e * n_tok * pitch, LANES), F32),
                        pltpu.VMEM((n_tok * pitch, LANES), F32),
                        pltpu.SemaphoreType.DMA((2,)),
                        pltpu.SMEM((2, 2), jnp.int32)],
        compiler_params=_params("arbitrary"),
        name="combine",
    )(cnt_blocks, src_blocks, cnt_blocks, src_blocks, x1_lin, ye_lin)


def _swap_halves(w):
    half = w.shape[-1] // 2
    return jnp.concatenate([w[..., half:], w[..., :half]], axis=-1)


def _prep_layer_params(p):
    d_lru = p["conv_w"].shape[-1]
    q_lora = p["q_norm"].shape[-1]
    kv_lora = p["kv_norm"].shape[-1]
    n_heads = p["w_q_up"].shape[-1] // QK_HEAD
    s3 = 2 * d_lru + q_lora + kv_lora
    w_in = p["w_in"]
    w_in_ext = jnp.concatenate([w_in, _swap_halves(w_in[:, s3:])], axis=-1).astype(BF16)
    wq = p["w_q_up"].reshape(q_lora, n_heads, QK_HEAD)
    wq_rope = wq[..., QK_NOPE:]
    wq_all = jnp.concatenate([wq[..., :QK_NOPE], wq_rope, _swap_halves(wq_rope)], axis=-1)
    wq_all = wq_all.reshape(q_lora, n_heads * HEAD_PAD).astype(BF16)

    def head_gain(g):
        g_rope = g[QK_NOPE:]
        return jnp.concatenate([g[:QK_NOPE], g_rope, _swap_halves(g_rope)]).reshape(1, HEAD_PAD)

    return dict(
        ln1=p["ln1"], w_in_ext=w_in_ext, conv_w=p["conv_w"], conv_b=p["conv_b"],
        w_gate=[jnp.concatenate([p["lru_w_a"][r], p["lru_w_x"][r]], axis=-1).astype(BF16)
                for r in range(2)],
        lru_b_a=p["lru_b_a"], lru_b_x=p["lru_b_x"], lru_lam=p["lru_lam"],
        q_norm=p["q_norm"], kv_norm=p["kv_norm"], wq_all=wq_all,
        wkv=p["w_kv_up"].astype(BF16), gq=head_gain(p["q_head_norm"]),
        gk=head_gain(p["k_head_norm"]), w_out=p["w_out"].astype(BF16), ln2=p["ln2"],
        w_router_t=p["w_router"].T, w1=p["w1"].astype(BF16), w3=p["w3"].astype(BF16),
        w2=p["w2"].astype(BF16), col_q=2 * d_lru, col_kv=2 * d_lru + q_lora, col_kr=s3)


def _rope_cs_table(s_len):
    pos = jnp.arange(s_len, dtype=F32)
    inv = ROPE_BASE ** (-jnp.arange(0, QK_ROPE, 2, dtype=F32) / QK_ROPE)
    ang = pos[:, None] * inv[None, :]
    cos, sin = jnp.cos(ang), jnp.sin(ang)
    return jnp.concatenate([cos, cos, -sin, sin], axis=-1)


def _layer(x, n_prompt_seq, p, cs_table):
    n_seq, s_len, d = x.shape
    rows = n_seq * s_len
    d_lru = p["conv_w"].shape[-1]

    h = norm_matmul(x.reshape(rows, d), p["ln1"], p["w_in_ext"], tm=_tile(rows, 256))
    h3 = h.reshape(n_seq, s_len, -1)

    t_len = _tile(s_len, 512)
    h_dir = [lru_scan(h3, p["conv_w"], p["conv_b"], p["w_gate"][r], p["lru_b_a"][r],
                      p["lru_b_x"][r], p["lru_lam"][r], t_len, reverse=bool(r))
             for r in range(2)]

    q, k, v = attn_prep(h3, cs_table, p["q_norm"], p["kv_norm"], p["wq_all"], p["wkv"],
                        p["gq"], p["gk"], _tile(s_len, 256), p["col_q"], p["col_kv"],
                        p["col_kr"])
    att = flash_attention(q, k, v, tq=_tile(s_len, 1024), tk=_tile(s_len, 1024))

    x1_lin, xn_lin, logits_t = mix_out(
        h_dir[0].reshape(rows, d_lru), h_dir[1].reshape(rows, d_lru), h,
        att.reshape(rows, -1), x.reshape(rows, d), p["w_out"], p["ln2"], p["w_router_t"],
        tm=_tile(rows, 256))

    bounds = (0, n_prompt_seq * s_len, rows)
    caps = [max(1, (CAPACITY * (hi - lo)) // N_EXPERTS) for lo, hi in zip(bounds, bounds[1:])]
    tm = _tile(min(caps), 512)
    idx_l, gate_l, src_l, cnt_l = [], [], [], []
    slot_offset = 0
    for (lo, hi), cap in zip(zip(bounds, bounds[1:]), caps):
        lg = logits_t[:, lo:hi].reshape(N_EXPERTS, (hi - lo) // LANES, LANES)
        idx, gate, src, cnt = route(lg, cap, row_offset=lo, slot_offset=slot_offset)
        slot_offset += N_EXPERTS * cap
        idx_l.append(idx.reshape(-1, 1, tm))
        gate_l.append(gate.reshape(-1, tm, 1))
        src_l.append(src.transpose(1, 0, 2))
        cnt_l.append(cnt[:, None, :])
    first_tiles = N_EXPERTS * (caps[0] // tm)

    def tile_expert(t):
        return jnp.where(t < first_tiles, t // (caps[0] // tm),
                         (t - first_tiles) // (caps[1] // tm))

    ye_lin = expert_ffn_gather(jnp.concatenate(idx_l), jnp.concatenate(gate_l), xn_lin, d,
                               p["w1"], p["w3"], p["w2"], tile_expert, tf=_tile(d, 512))
    x2 = combine(jnp.concatenate(cnt_l), jnp.concatenate(src_l), x1_lin, ye_lin, d)
    return x2.reshape(n_seq, s_len, d)


def kernel(x_prompt, x_sample, ln1, w_in, conv_w, conv_b, lru_w_a, lru_b_a, lru_w_x, lru_b_x,
           lru_lam, q_norm, w_q_up, kv_norm, w_kv_up, q_head_norm, k_head_norm, w_out, ln2,
           w_router, w1, w3, w2, final_norm):
    depth = ln1.shape[0]
    n_prompt_seq = x_prompt.shape[0]
    x = jnp.concatenate([x_prompt, x_sample], axis=0)
    n_seq, s_len, d = x.shape
    cs_table = _rope_cs_table(s_len)
    for l in range(depth):
        p = _prep_layer_params(dict(
            ln1=ln1[l], w_in=w_in[l], conv_w=conv_w[l], conv_b=conv_b[l], lru_w_a=lru_w_a[l],
            lru_b_a=lru_b_a[l], lru_w_x=lru_w_x[l], lru_b_x=lru_b_x[l], lru_lam=lru_lam[l],
            q_norm=q_norm[l], w_q_up=w_q_up[l], kv_norm=kv_norm[l], w_kv_up=w_kv_up[l],
            q_head_norm=q_head_norm[l], k_head_norm=k_head_norm[l], w_out=w_out[l], ln2=ln2[l],
            w_router=w_router[l], w1=w1[l], w3=w3[l], w2=w2[l]))
        x = _layer(x, n_prompt_seq, p, cs_table)
    y = rms_norm_rows(x.reshape(n_seq * s_len, d), final_norm, _tile(n_seq * s_len, 512))
    y = y.reshape(n_seq, s_len, d)
    return y[:n_prompt_seq], y[n_prompt_seq:]
```

```python
import functools
import math

import jax
import jax.numpy as jnp
from jax import lax
from jax.experimental import pallas as pl
from jax.experimental.pallas import tpu as pltpu

F32 = jnp.float32
BF16 = jnp.bfloat16

EPS = 1e-6
LRU_BLOCKS = 8
CONV_WIDTH = 4
LRU_C = 8.0
V_HEAD_DIM = 128
QK_NOPE = 128
QK_ROPE = 64
QK_HEAD = QK_NOPE + QK_ROPE
ROPE_BASE = 10000.0
N_EXPERTS = 16
CAPACITY = 2

LANES = 128
SUBLANES = 8
HEAD_PAD = 2 * LANES
V7X_VMEM_LIMIT_BYTES = 56 * 1024 * 1024
LOG2E = math.log2(math.e)
COMBINE_STATIC_K = 3


def _tile(n, pref):
    t = min(n, pref)
    while n % t:
        t //= 2
    return t


def _params(*semantics):
    return pltpu.CompilerParams(dimension_semantics=semantics,
                                vmem_limit_bytes=V7X_VMEM_LIMIT_BYTES)


def _rms(x):
    return lax.rsqrt(jnp.mean(x * x, axis=-1, keepdims=True) + EPS)


def _store_row_linear(ref, base, x):
    n_rows, d = x.shape
    pitch = d // LANES
    for c in range(pitch):
        ref[pl.ds(base + c, n_rows, stride=pitch), :] = x[:, c * LANES:(c + 1) * LANES]


def _load_row_linear(ref, base, n_rows, pitch):
    return jnp.concatenate(
        [ref[pl.ds(base + c, n_rows, stride=pitch), :] for c in range(pitch)], axis=1)


def _norm_matmul_kernel(x_ref, g_ref, w_ref, o_ref):
    x = x_ref[...]
    y = x * _rms(x) * g_ref[...]
    o_ref[...] = jnp.dot(y.astype(BF16), w_ref[...], preferred_element_type=F32)


def norm_matmul(x, g, w_bf16, tm):
    rows, d = x.shape
    n = w_bf16.shape[1]
    return pl.pallas_call(
        _norm_matmul_kernel,
        out_shape=jax.ShapeDtypeStruct((rows, n), F32),
        grid=(rows // tm,),
        in_specs=[pl.BlockSpec((tm, d), lambda i: (i, 0)),
                  pl.BlockSpec((1, d), lambda i: (0, 0)),
                  pl.BlockSpec((d, n), lambda i: (0, 0))],
        out_specs=pl.BlockSpec((tm, n), lambda i: (i, 0)),
        compiler_params=_params("parallel"),
        name="norm_matmul",
    )(x, g.reshape(1, d), w_bf16)


def _norm_kernel(x_ref, g_ref, o_ref):
    x = x_ref[...]
    o_ref[...] = x * _rms(x) * g_ref[...]


def rms_norm_rows(x, g, tm):
    rows, d = x.shape
    return pl.pallas_call(
        _norm_kernel,
        out_shape=jax.ShapeDtypeStruct((rows, d), F32),
        grid=(rows // tm,),
        in_specs=[pl.BlockSpec((tm, d), lambda i: (i, 0)),
                  pl.BlockSpec((1, d), lambda i: (0, 0))],
        out_specs=pl.BlockSpec((tm, d), lambda i: (i, 0)),
        compiler_params=_params("parallel"),
        name="final_norm",
    )(x, g.reshape(1, d))


def _lru_kernel(x_ref, prev_ref, next_ref, cw_ref, cb_ref, wg_ref, ba_ref, bx_ref, lam_ref,
                o_ref, ext_sc, a_sc, u_sc, carry_sc, *, reverse):
    c = pl.program_id(1)
    n_c = pl.num_programs(1)
    t_len, ch = a_sc.shape
    blk = ch // LRU_BLOCKS
    first_in_time = (c == n_c - 1) if reverse else (c == 0)
    last_in_time = (c == 0) if reverse else (c == n_c - 1)

    @pl.when(c == 0)
    def _():
        carry_sc[...] = jnp.zeros(carry_sc.shape, F32)

    ext_sc[0:SUBLANES, :] = jnp.where(first_in_time, 0.0, prev_ref[0])
    ext_sc[SUBLANES:SUBLANES + t_len, :] = x_ref[0]
    ext_sc[SUBLANES + t_len:, :] = jnp.where(last_in_time, 0.0, next_ref[0])
    xc = cb_ref[...]
    for tap in range(CONV_WIDTH):
        off = SUBLANES + tap - 2
        xc = xc + cw_ref[tap:tap + 1, :] * ext_sc[off:off + t_len, :]

    lam = lam_ref[...]
    log_sig_lam = jnp.minimum(lam, 0.0) - jnp.log1p(jnp.exp(-jnp.abs(lam)))
    for n in range(LRU_BLOCKS):
        sl = slice(n * blk, (n + 1) * blk)
        xcn = xc[:, sl]
        g = jnp.dot(xcn.astype(BF16), wg_ref[n], preferred_element_type=F32)
        r = jax.nn.sigmoid(g[:, :blk] + ba_ref[:, sl])
        i = jax.nn.sigmoid(g[:, blk:] + bx_ref[:, sl])
        log_a = LRU_C * r * log_sig_lam[:, sl]
        a_sc[:, sl] = jnp.exp(log_a)
        u_sc[:, sl] = jnp.sqrt(1.0 - jnp.exp(2.0 * log_a)) * (i * xcn)

    row = lax.broadcasted_iota(jnp.int32, (SUBLANES, ch), 0)
    n_groups = t_len // SUBLANES

    def group(gi, carry):
        g0 = (n_groups - 1 - gi) if reverse else gi
        start = pl.multiple_of(g0 * SUBLANES, SUBLANES)
        a = a_sc[pl.ds(start, SUBLANES), :]
        u = u_sc[pl.ds(start, SUBLANES), :]
        for d in (1, 2, 4):
            shift = SUBLANES - d if reverse else d
            a_s = pltpu.roll(a, shift, 0)
            u_s = pltpu.roll(u, shift, 0)
            valid = (row < SUBLANES - d) if reverse else (row >= d)
            u = jnp.where(valid, a * u_s + u, u)
            a = jnp.where(valid, a * a_s, a)
        h = u + a * carry
        o_ref[0, pl.ds(start, SUBLANES), :] = h
        edge = h[0:1, :] if reverse else h[SUBLANES - 1:SUBLANES, :]
        return jnp.broadcast_to(edge, (SUBLANES, ch))

    carry_sc[...] = lax.fori_loop(0, n_groups, group, carry_sc[...], unroll=2)


def lru_scan(h3, conv_w, conv_b, w_gate, b_a, b_x, lam, t_len, reverse):
    n_seq, s_len, _ = h3.shape
    ch = conv_w.shape[-1]
    n_c = s_len // t_len
    per = t_len // SUBLANES
    n_sub = s_len // SUBLANES
    cidx = (lambda c: n_c - 1 - c) if reverse else (lambda c: c)
    return pl.pallas_call(
        functools.partial(_lru_kernel, reverse=reverse),
        out_shape=jax.ShapeDtypeStruct((n_seq, s_len, ch), F32),
        grid=(n_seq, n_c),
        in_specs=[
            pl.BlockSpec((1, t_len, ch), lambda s, c: (s, cidx(c), 0)),
            pl.BlockSpec((1, SUBLANES, ch),
                         lambda s, c: (s, jnp.maximum(cidx(c) * per - 1, 0), 0)),
            pl.BlockSpec((1, SUBLANES, ch),
                         lambda s, c: (s, jnp.minimum((cidx(c) + 1) * per, n_sub - 1), 0)),
            pl.BlockSpec((CONV_WIDTH, ch), lambda s, c: (0, 0)),
            pl.BlockSpec((1, ch), lambda s, c: (0, 0)),
            pl.BlockSpec(w_gate.shape, lambda s, c: (0, 0, 0)),
            pl.BlockSpec((1, ch), lambda s, c: (0, 0)),
            pl.BlockSpec((1, ch), lambda s, c: (0, 0)),
            pl.BlockSpec((1, ch), lambda s, c: (0, 0)),
        ],
        out_specs=pl.BlockSpec((1, t_len, ch), lambda s, c: (s, cidx(c), 0)),
        scratch_shapes=[pltpu.VMEM((t_len + 2 * SUBLANES, ch), F32),
                        pltpu.VMEM((t_len, ch), F32),
                        pltpu.VMEM((t_len, ch), F32),
                        pltpu.VMEM((SUBLANES, ch), F32)],
        compiler_params=_params("parallel", "arbitrary"),
        name="lru_rev" if reverse else "lru_fwd",
    )(h3, h3, h3, conv_w, conv_b.reshape(1, ch), w_gate, b_a.reshape(1, ch),
      b_x.reshape(1, ch), lam.reshape(1, ch))


def _attn_prep_kernel(cq_ref, ckv_ref, kr_ref, cs_ref, qn_ref, kvn_ref, wq_ref, wkv_ref,
                      gq_ref, gk_ref, q_out, k_out, v_out, *, n_heads, q_scale):
    cq = cq_ref[...]
    ckv = ckv_ref[...]
    qa = jnp.dot((cq * _rms(cq) * qn_ref[...]).astype(BF16), wq_ref[...],
                 preferred_element_type=F32)
    kva = jnp.dot((ckv * _rms(ckv) * kvn_ref[...]).astype(BF16), wkv_ref[...],
                  preferred_element_type=F32)
    cs = cs_ref[...]
    lane = lax.broadcasted_iota(jnp.int32, cs.shape, 1)
    rope_lanes = lane < QK_ROPE
    gq = gq_ref[...]
    gk = gk_ref[...]
    vk = kr_ref[...]
    k_rope_ss = 0.5 * jnp.sum(vk * vk, axis=-1, keepdims=True)
    bk = vk * gk[:, LANES:] * cs
    bk = bk + pltpu.roll(bk, QK_ROPE, 1)
    ones = jnp.ones(cs.shape, BF16)
    for h in range(n_heads):
        base = h * HEAD_PAD
        qn = qa[:, base:base + LANES]
        qv = qa[:, base + LANES:base + HEAD_PAD]
        ss = jnp.sum(qn * qn, axis=-1, keepdims=True) + 0.5 * jnp.sum(qv * qv, axis=-1,
                                                                     keepdims=True)
        rs = lax.rsqrt(ss * (1.0 / QK_HEAD) + EPS) * q_scale
        bq = qv * gq[:, LANES:] * cs
        bq = bq + pltpu.roll(bq, QK_ROPE, 1)
        q_out[0, h, :, 0:LANES] = (qn * gq[:, :LANES] * rs).astype(BF16)
        q_out[0, h, :, LANES:] = (bq * rs).astype(BF16)

        kn = kva[:, base:base + LANES]
        ss = jnp.sum(kn * kn, axis=-1, keepdims=True) + k_rope_ss
        rs = lax.rsqrt(ss * (1.0 / QK_HEAD) + EPS)
        k_out[0, h, :, 0:LANES] = (kn * gk[:, :LANES] * rs).astype(BF16)
        k_out[0, h, :, LANES:] = jnp.where(rope_lanes, bk * rs, 0.0).astype(BF16)

        v_out[0, h, :, 0:LANES] = kva[:, base + LANES:base + HEAD_PAD].astype(BF16)
        v_out[0, h, :, LANES:] = ones


def attn_prep(h3, cs_table, q_norm, kv_norm, wq_all, wkv, gq, gk, tm, col_q, col_kv, col_kr):
    n_seq, s_len, _ = h3.shape
    q_lora = q_norm.shape[-1]
    kv_lora = kv_norm.shape[-1]
    n_heads = wq_all.shape[-1] // HEAD_PAD
    q_scale = QK_HEAD ** -0.5 * LOG2E
    out = jax.ShapeDtypeStruct((n_seq, n_heads, s_len, HEAD_PAD), BF16)
    head_spec = pl.BlockSpec((1, n_heads, tm, HEAD_PAD), lambda s, i: (s, 0, i, 0))
    const2 = lambda s, i: (0, 0)
    return pl.pallas_call(
        functools.partial(_attn_prep_kernel, n_heads=n_heads, q_scale=q_scale),
        out_shape=(out, out, out),
        grid=(n_seq, s_len // tm),
        in_specs=[
            pl.BlockSpec((None, tm, q_lora), lambda s, i: (s, i, col_q // q_lora)),
            pl.BlockSpec((None, tm, kv_lora), lambda s, i: (s, i, col_kv // kv_lora)),
            pl.BlockSpec((None, tm, LANES), lambda s, i: (s, i, col_kr // LANES)),
            pl.BlockSpec((tm, LANES), lambda s, i: (i, 0)),
            pl.BlockSpec((1, q_lora), const2),
            pl.BlockSpec((1, kv_lora), const2),
            pl.BlockSpec(wq_all.shape, const2),
            pl.BlockSpec(wkv.shape, const2),
            pl.BlockSpec((1, HEAD_PAD), const2),
            pl.BlockSpec((1, HEAD_PAD), const2),
        ],
        out_specs=(head_spec, head_spec, head_spec),
        compiler_params=_params("parallel", "parallel"),
        name="attn_prep",
    )(h3, h3, h3, cs_table, q_norm.reshape(1, -1), kv_norm.reshape(1, -1), wq_all, wkv, gq, gk)


def _flash_kernel(q_ref, k_ref, v_ref, o_ref, s_a, s_b, p_a, p_b, al_a, al_b, m_sc, acc_sc,
                  *, tk, n_kv):
    q = q_ref[0, 0]
    m_sc[...] = jnp.full(m_sc.shape, -jnp.inf, F32)
    acc_sc[...] = jnp.zeros(acc_sc.shape, F32)

    def scores(j, s_ref):
        k = k_ref[0, 0, j * tk:(j + 1) * tk, :]
        s_ref[...] = lax.dot_general(q, k, (((1,), (1,)), ((), ())),
                                     preferred_element_type=F32)

    def softmax(s_ref, p_ref, al_ref):
        s = s_ref[...]
        m_prev = m_sc[...]
        m_new = jnp.maximum(m_prev, jnp.max(s, axis=-1, keepdims=True))
        al_ref[...] = jnp.exp2(m_prev - m_new)
        p_ref[...] = jnp.exp2(s - m_new[:, :1]).astype(BF16)
        m_sc[...] = m_new

    def values(j, p_ref, al_ref):
        v = v_ref[0, 0, j * tk:(j + 1) * tk, :]
        pv = jnp.dot(p_ref[...], v, preferred_element_type=F32)
        alpha = al_ref[...]
        acc_sc[:, :LANES] = alpha * acc_sc[:, :LANES] + pv[:, :LANES]
        acc_sc[:, LANES:] = alpha * acc_sc[:, LANES:] + pv[:, LANES:]

    bufs = ((s_a, p_a, al_a), (s_b, p_b, al_b))
    scores(0, s_a)
    for j in range(n_kv):
        s_cur, p_cur, al_cur = bufs[j % 2]
        s_nxt, p_prev, al_prev = bufs[(j + 1) % 2]
        if j + 1 < n_kv:
            scores(j + 1, s_nxt)
        softmax(s_cur, p_cur, al_cur)
        if j >= 1:
            values(j - 1, p_prev, al_prev)
    _, p_last, al_last = bufs[(n_kv - 1) % 2]
    values(n_kv - 1, p_last, al_last)
    o_ref[0] = (acc_sc[:, :LANES] / acc_sc[:, LANES:]).astype(o_ref.dtype)


def flash_attention(q, k, v, tq, tk):
    n_seq, n_heads, s_len, dq = q.shape
    n_kv = s_len // tk
    kv_spec = pl.BlockSpec((1, 1, s_len, dq), lambda s, h, i: (s, h, 0, 0))
    return pl.pallas_call(
        functools.partial(_flash_kernel, tk=tk, n_kv=n_kv),
        out_shape=jax.ShapeDtypeStruct((n_seq, s_len, n_heads * V_HEAD_DIM), BF16),
        grid=(n_seq, n_heads, s_len // tq),
        in_specs=[pl.BlockSpec((1, 1, tq, dq), lambda s, h, i: (s, h, i, 0)), kv_spec, kv_spec],
        out_specs=pl.BlockSpec((1, tq, V_HEAD_DIM), lambda s, h, i: (s, i, h)),
        scratch_shapes=[pltpu.VMEM((tq, tk), F32), pltpu.VMEM((tq, tk), F32),
                        pltpu.VMEM((tq, tk), BF16), pltpu.VMEM((tq, tk), BF16),
                        pltpu.VMEM((tq, LANES), F32), pltpu.VMEM((tq, LANES), F32),
                        pltpu.VMEM((tq, LANES), F32), pltpu.VMEM((tq, dq), F32)],
        compiler_params=_params("parallel", "parallel", "arbitrary"),
        name="flash_attention",
    )(q, k, v)


def _mix_out_kernel(hf_ref, hb_ref, y_ref, att_ref, x_ref, w_ref, g_ref, wr_ref,
                    x1_ref, xn_ref, lg_ref):
    ch = hf_ref.shape[-1]
    y = y_ref[...]
    gelu = 0.5 * y * (1.0 + jnp.tanh(math.sqrt(2.0 / math.pi) * (y + 0.044715 * (y * y * y))))
    rnn = ((hf_ref[...] + hb_ref[...]) * gelu).astype(BF16)
    x1 = (x_ref[...]
          + jnp.dot(rnn, w_ref[0:ch, :], preferred_element_type=F32)
          + jnp.dot(att_ref[...], w_ref[ch:, :], preferred_element_type=F32))
    xn = x1 * _rms(x1) * g_ref[...]
    _store_row_linear(x1_ref, 0, x1)
    _store_row_linear(xn_ref, 0, xn)
    wr = wr_ref[...]
    x_hi = xn.astype(BF16)
    x_lo = (xn - x_hi.astype(F32)).astype(BF16)
    both = jnp.dot(x_hi, wr, preferred_element_type=F32)
    lo = jnp.dot(x_lo, wr[:, :LANES], preferred_element_type=F32)
    lg = both[:, :LANES] + both[:, LANES:] + lo
    lg_ref[...] = jnp.transpose(lg)[0:lg_ref.shape[0], :]


def mix_out(hf, hb, h2, att, x, w_out, ln2, w_router_split, n_e, tm):
    rows, ch = hf.shape
    d = x.shape[-1]
    pitch = d // LANES
    row_spec = lambda width: pl.BlockSpec((tm, width), lambda i: (i, 0))
    lin_spec = pl.BlockSpec((tm * pitch, LANES), lambda i: (i, 0))
    lin_shape = jax.ShapeDtypeStruct((rows * pitch, LANES), F32)
    return pl.pallas_call(
        _mix_out_kernel,
        out_shape=(lin_shape, lin_shape, jax.ShapeDtypeStruct((n_e, rows), F32)),
        grid=(rows // tm,),
        in_specs=[row_spec(ch), row_spec(ch),
                  pl.BlockSpec((tm, ch), lambda i: (i, 1)),
                  row_spec(att.shape[-1]), row_spec(d),
                  pl.BlockSpec(w_out.shape, lambda i: (0, 0)),
                  pl.BlockSpec((1, d), lambda i: (0, 0)),
                  pl.BlockSpec(w_router_split.shape, lambda i: (0, 0))],
        out_specs=(lin_spec, lin_spec, pl.BlockSpec((n_e, tm), lambda i: (0, i))),
        compiler_params=_params("parallel"),
        name="mix_out",
    )(hf, hb, h2, att, x, w_out, ln2.reshape(1, d), w_router_split)


def _sum_all(x):
    return jnp.sum(jnp.sum(x, axis=0, keepdims=True), axis=1, keepdims=True)


def _route_kernel(lg_ref, idx_ref, gate_ref, src_ref, cnt_ref, kmax_ref, *, cap, row_offset,
                  slot_offset, s_chunk):
    e = pl.program_id(0)
    n_e, nb, _ = lg_ref.shape
    lg = lg_ref[...]
    m = jnp.max(lg, axis=0)
    den = jnp.sum(jnp.exp(lg - m[None]), axis=0)
    aff = jnp.exp(lg_ref[e] - m) / den
    bits = pltpu.bitcast(aff, jnp.int32)

    def bit_step(i, prefix):
        cand = prefix | jnp.left_shift(jnp.int32(1), 30 - i)
        n_ge = _sum_all(jnp.where(bits >= cand, 1.0, 0.0))
        return jnp.where(n_ge >= cap, cand, prefix)

    thr = lax.fori_loop(0, 31, bit_step, jnp.zeros((1, 1), jnp.int32))

    r128 = lax.broadcasted_iota(jnp.int32, (LANES, LANES), 0)
    c128 = lax.broadcasted_iota(jnp.int32, (LANES, LANES), 1)
    upper_lanes = jnp.where(r128 <= c128, 1.0, 0.0).astype(BF16)
    rb = lax.broadcasted_iota(jnp.int32, (nb, nb), 0)
    cb = lax.broadcasted_iota(jnp.int32, (nb, nb), 1)
    lower_rows = jnp.where(cb < rb, 1.0, 0.0).astype(BF16)
    upper_rows = jnp.where(rb <= cb, 1.0, 0.0).astype(BF16)

    def cumsum_tokens(mf):
        row_cum = jnp.dot(mf.astype(BF16), upper_lanes, preferred_element_type=F32)
        tot = jnp.broadcast_to(row_cum[:, LANES - 1:LANES], (nb, LANES))
        return row_cum + jnp.dot(lower_rows, tot.astype(BF16), preferred_element_type=F32)

    gt = bits > thr
    eq = bits == thr
    eq_f = jnp.where(eq, 1.0, 0.0)
    need = cap - _sum_all(jnp.where(gt, 1.0, 0.0))
    eq_rank = cumsum_tokens(eq_f) - eq_f
    sel = gt | (eq & (eq_rank < need))
    sel_f = jnp.where(sel, 1.0, 0.0)
    cum = cumsum_tokens(sel_f)

    @pl.when(e == 0)
    def _():
        cnt_ref[...] = jnp.zeros(cnt_ref.shape, jnp.int32)
        src_ref[...] = jnp.zeros(src_ref.shape, jnp.int32)

    rank = cnt_ref[...]
    flat = (cum - sel_f).astype(jnp.int32) + (slot_offset + e * cap)
    for k in range(n_e):
        src_ref[k] = jnp.where(sel & (rank == k), flat, src_ref[k])
    cnt_new = rank + sel.astype(jnp.int32)
    cnt_ref[...] = cnt_new
    row_max = jnp.max(cnt_new.astype(F32), axis=1, keepdims=True)
    kmax_ref[...] = jnp.broadcast_to(row_max, kmax_ref.shape).astype(jnp.int32)

    tot_lane = lax.dot_general(jnp.ones((SUBLANES, LANES), BF16), sel_f.astype(BF16),
                               (((1,), (1,)), ((), ())), preferred_element_type=F32)
    end_incl = jnp.dot(tot_lane.astype(BF16), upper_rows, preferred_element_type=F32)[0:1]
    end_excl = end_incl - tot_lane[0:1]
    cum_hi = jnp.floor(cum * (1.0 / 64.0))
    a1 = aff.astype(BF16)
    r1 = aff - a1.astype(F32)
    a2 = r1.astype(BF16)
    a3 = (r1 - a2.astype(F32)).astype(BF16)
    payload = jnp.concatenate(
        [cum_hi.astype(BF16), (cum - 64.0 * cum_hi).astype(BF16), a1, a2, a3], axis=1)
    row_id = lax.broadcasted_iota(jnp.int32, (s_chunk, nb), 1).astype(F32)
    lane_id = lax.broadcasted_iota(jnp.int32, (s_chunk, LANES), 1)

    def chunk(c, carry):
        base = pl.multiple_of(c * s_chunk, s_chunk)
        s_col = (lax.broadcasted_iota(jnp.int32, (s_chunk, 1), 0) + base).astype(F32)
        hit = (end_excl <= s_col) & (s_col < end_incl)
        hit_f = jnp.where(hit, 1.0, 0.0)
        got = jnp.dot(hit_f.astype(BF16), payload, preferred_element_type=F32)
        cum_row = 64.0 * got[:, 0:LANES] + got[:, LANES:2 * LANES]
        aff_row = got[:, 2 * LANES:3 * LANES] + got[:, 3 * LANES:4 * LANES] + got[:, 4 * LANES:]
        j = jnp.sum(jnp.where(cum_row <= s_col, 1.0, 0.0), axis=1, keepdims=True)
        blk = jnp.sum(hit_f * row_id, axis=1, keepdims=True)
        idx_ref[0, pl.ds(base, s_chunk), :] = (blk * LANES + j).astype(jnp.int32) + row_offset
        gate_ref[0, pl.ds(base, s_chunk), :] = jnp.sum(
            jnp.where(lane_id == j.astype(jnp.int32), aff_row, 0.0), axis=1, keepdims=True)
        return carry

    lax.fori_loop(0, cap // s_chunk, chunk, 0)


def route(lg, cap, row_offset, slot_offset):
    n_e, nb, _ = lg.shape
    assert cap <= 64 * 256, "slot counts are split into two bf16-exact digits"
    s_chunk = _tile(cap, 512)
    return pl.pallas_call(
        functools.partial(_route_kernel, cap=cap, row_offset=row_offset,
                          slot_offset=slot_offset, s_chunk=s_chunk),
        out_shape=(jax.ShapeDtypeStruct((n_e, cap, 1), jnp.int32),
                   jax.ShapeDtypeStruct((n_e, cap, 1), F32),
                   jax.ShapeDtypeStruct((n_e, nb, LANES), jnp.int32),
                   jax.ShapeDtypeStruct((nb, LANES), jnp.int32),
                   jax.ShapeDtypeStruct((nb, LANES), jnp.int32)),
        grid=(n_e,),
        in_specs=[pl.BlockSpec((n_e, nb, LANES), lambda e: (0, 0, 0))],
        out_specs=(pl.BlockSpec((1, cap, 1), lambda e: (e, 0, 0)),
                   pl.BlockSpec((1, cap, 1), lambda e: (e, 0, 0)),
                   pl.BlockSpec((n_e, nb, LANES), lambda e: (0, 0, 0)),
                   pl.BlockSpec((nb, LANES), lambda e: (0, 0)),
                   pl.BlockSpec((nb, LANES), lambda e: (0, 0))),
        compiler_params=_params("arbitrary"),
        name="route",
    )(lg)


def _ffn_gather_kernel(idx_ref, idx_next_ref, gate_ref, xn_hbm, w1_ref, w3_ref, w2_ref,
                       o_ref, xbuf, xs, acc, sem, *, n_f):
    t = pl.program_id(0)
    f = pl.program_id(1)
    n_t = pl.num_programs(0)
    tm, d = xs.shape
    pitch = d // LANES
    slot = t % 2
    rows_per_step = tm // n_f

    def row_copy(idx_smem, r, slot_):
        src = pl.multiple_of(idx_smem[0, 0, r] * pitch, pitch)
        dst = pl.multiple_of((slot_ * tm + r) * pitch, pitch)
        return pltpu.make_async_copy(xn_hbm.at[pl.ds(src, pitch), :],
                                     xbuf.at[pl.ds(dst, pitch), :], sem.at[slot_])

    def start_rows(idx_smem, slot_):
        def body(r, carry):
            row_copy(idx_smem, r, slot_).start()
            return carry
        lax.fori_loop(0, tm, body, 0, unroll=8)

    def wait_slot(slot_):
        base = pl.multiple_of(slot_ * (tm * pitch), SUBLANES)
        pltpu.make_async_copy(xn_hbm.at[pl.ds(0, tm * pitch), :],
                              xbuf.at[pl.ds(base, tm * pitch), :], sem.at[slot_]).wait()

    @pl.when((t == 0) & (f == 0))
    def _():
        start_rows(idx_ref, 0)

    for r in range(rows_per_step):
        row_copy(idx_next_ref, f * rows_per_step + r, 1 - slot).start()

    @pl.when(f == 0)
    def _():
        wait_slot(slot)
        base = pl.multiple_of(slot * (tm * pitch), SUBLANES)
        xs[...] = _load_row_linear(xbuf, base, tm, pitch).astype(BF16)

    x = xs[...]
    a = jnp.dot(x, w1_ref[0], preferred_element_type=F32)
    b = jnp.dot(x, w3_ref[0], preferred_element_type=F32)
    hid = (a * jax.nn.sigmoid(a)) * b
    y = jnp.dot(hid.astype(BF16), w2_ref[0], preferred_element_type=F32)

    @pl.when(f == 0)
    def _():
        acc[...] = y

    @pl.when(f > 0)
    def _():
        acc[...] += y

    @pl.when(f == n_f - 1)
    def _():
        _store_row_linear(o_ref, 0, acc[...] * gate_ref[0])

        @pl.when(t == n_t - 1)
        def _():
            wait_slot(1 - slot)


def expert_ffn_gather(idx_tiles, gate_tiles, xn_lin, d, w1, w3, w2, tile_expert, tf):
    n_tiles, _, tm = idx_tiles.shape
    pitch = d // LANES
    d_ff = w1.shape[-1]
    n_f = d_ff // tf
    assert tm % n_f == 0
    smem = pltpu.MemorySpace.SMEM
    return pl.pallas_call(
        functools.partial(_ffn_gather_kernel, n_f=n_f),
        out_shape=jax.ShapeDtypeStruct((n_tiles * tm * pitch, LANES), F32),
        grid=(n_tiles, d_ff // tf),
        in_specs=[pl.BlockSpec((1, 1, tm), lambda t, f: (t, 0, 0), memory_space=smem),
                  pl.BlockSpec((1, 1, tm), lambda t, f: (jnp.minimum(t + 1, n_tiles - 1), 0, 0),
                               memory_space=smem),
                  pl.BlockSpec((1, tm, 1), lambda t, f: (t, 0, 0)),
                  pl.BlockSpec(memory_space=pl.ANY),
                  pl.BlockSpec((1, d, tf), lambda t, f: (tile_expert(t), 0, f)),
                  pl.BlockSpec((1, d, tf), lambda t, f: (tile_expert(t), 0, f)),
                  pl.BlockSpec((1, tf, d), lambda t, f: (tile_expert(t), f, 0))],
        out_specs=pl.BlockSpec((tm * pitch, LANES), lambda t, f: (t, 0)),
        scratch_shapes=[pltpu.VMEM((2 * tm * pitch, LANES), F32),
                        pltpu.VMEM((tm, d), BF16),
                        pltpu.VMEM((tm, d), F32),
                        pltpu.SemaphoreType.DMA((2,))],
        compiler_params=_params("arbitrary", "arbitrary"),
        name="expert_ffn",
    )(idx_tiles, idx_tiles, gate_tiles, xn_lin, w1, w3, w2)


def _combine_kernel(cnt_ref, src_ref, cnt_next_ref, src_next_ref, x1_ref, ye_hbm, o_ref,
                    buf, acc, sem, meta):
    i = pl.program_id(0)
    n_i = pl.num_programs(0)
    slot = i % 2
    n_tok, d = o_ref.shape
    pitch = d // LANES
    n_e = src_ref.shape[1]
    slab = n_tok * pitch

    def row_copy(row, k, t, slot_):
        src = pl.multiple_of(row * pitch, pitch)
        dst = pl.multiple_of(((slot_ * n_e + k) * n_tok + t) * pitch, pitch)
        return pltpu.make_async_copy(ye_hbm.at[pl.ds(src, pitch), :],
                                     buf.at[pl.ds(dst, pitch), :], sem.at[slot_])

    def slab_start(k, slot_):
        return pl.multiple_of((slot_ * n_e + k) * slab, SUBLANES)

    def fetch(cnt_smem, src_smem, slot_):
        k_max = cnt_smem[0, 1, 0]

        def clear(k, carry):
            buf[pl.ds(slab_start(k, slot_), slab), :] = jnp.zeros((slab, LANES), F32)
            return carry
        lax.fori_loop(0, k_max, clear, 0)

        def token(t, total):
            c = cnt_smem[0, 0, t]

            def one(k, carry):
                row_copy(src_smem[0, k, t], k, t, slot_).start()
                return carry

            for k in range(COMBINE_STATIC_K):
                @pl.when(c > k)
                def _():
                    one(k, 0)

            @pl.when(c > COMBINE_STATIC_K)
            def _():
                lax.fori_loop(COMBINE_STATIC_K, c, one, 0)
            return total + c
        meta[slot_, 0] = lax.fori_loop(0, n_tok, token, 0)
        meta[slot_, 1] = k_max

    @pl.when(i == 0)
    def _():
        fetch(cnt_ref, src_ref, 0)

    @pl.when(i + 1 < n_i)
    def _():
        fetch(cnt_next_ref, src_next_ref, 1 - slot)

    total = meta[slot, 0]
    for bit in range((n_e * n_tok).bit_length()):
        @pl.when(((total >> bit) & 1) == 1)
        def _():
            n_rows = (1 << bit) * pitch
            pltpu.make_async_copy(ye_hbm.at[pl.ds(0, n_rows), :],
                                  buf.at[pl.ds(slab_start(0, slot), n_rows), :],
                                  sem.at[slot]).wait()

    acc[...] = x1_ref[...]

    def add(k, carry):
        acc[...] += buf[pl.ds(slab_start(k, slot), slab), :]
        return carry
    lax.fori_loop(0, meta[slot, 1], add, 0)
    o_ref[...] = _load_row_linear(acc, 0, n_tok, pitch)


def combine(cnt_blocks, src_blocks, x1_lin, ye_lin, d):
    n_blk, n_e, n_tok = src_blocks.shape
    pitch = d // LANES
    smem = pltpu.MemorySpace.SMEM
    nxt = lambda i: jnp.minimum(i + 1, n_blk - 1)
    return pl.pallas_call(
        _combine_kernel,
        out_shape=jax.ShapeDtypeStruct((n_blk * n_tok, d), F32),
        grid=(n_blk,),
        in_specs=[pl.BlockSpec((1, 2, n_tok), lambda i: (i, 0, 0), memory_space=smem),
                  pl.BlockSpec((1, n_e, n_tok), lambda i: (i, 0, 0), memory_space=smem),
                  pl.BlockSpec((1, 2, n_tok), lambda i: (nxt(i), 0, 0), memory_space=smem),
                  pl.BlockSpec((1, n_e, n_tok), lambda i: (nxt(i), 0, 0), memory_space=smem),
                  pl.BlockSpec((n_tok * pitch, LANES), lambda i: (i, 0)),
                  pl.BlockSpec(memory_space=pl.ANY)],
        out_specs=pl.BlockSpec((n_tok, d), lambda i: (i, 0)),
        scratch_shapes=[pltpu.VMEM((2 * n_e * n_tok * pitch, LANES), F32),
                        pltpu.VMEM((n_tok * pitch, LANES), F32),
                        pltpu.SemaphoreType.DMA((2,)),
                        pltpu.SMEM((2, 2), jnp.int32)],
        compiler_params=_params("arbitrary"),
        name="combine",
    )(cnt_blocks, src_blocks, cnt_blocks, src_blocks, x1_lin, ye_lin)


def _swap_halves(w):
    half = w.shape[-1] // 2
    return jnp.concatenate([w[..., half:], w[..., :half]], axis=-1)


def _prep_layer_params(p):
    d_lru = p["conv_w"].shape[-1]
    q_lora = p["q_norm"].shape[-1]
    kv_lora = p["kv_norm"].shape[-1]
    n_heads = p["w_q_up"].shape[-1] // QK_HEAD
    s3 = 2 * d_lru + q_lora + kv_lora
    w_in = p["w_in"]
    w_in_ext = jnp.concatenate([w_in, _swap_halves(w_in[:, s3:])], axis=-1).astype(BF16)
    wq = p["w_q_up"].reshape(q_lora, n_heads, QK_HEAD)
    wq_rope = wq[..., QK_NOPE:]
    wq_all = jnp.concatenate([wq[..., :QK_NOPE], wq_rope, _swap_halves(wq_rope)], axis=-1)
    wq_all = wq_all.reshape(q_lora, n_heads * HEAD_PAD).astype(BF16)

    def head_gain(g):
        g_rope = g[QK_NOPE:]
        return jnp.concatenate([g[:QK_NOPE], g_rope, _swap_halves(g_rope)]).reshape(1, HEAD_PAD)

    w_r = p["w_router"]
    w_r_hi = w_r.astype(BF16)
    w_r_lo = (w_r - w_r_hi.astype(F32)).astype(BF16)
    lane_pad = ((0, 0), (0, LANES - w_r.shape[-1]))
    w_router_split = jnp.concatenate([jnp.pad(w_r_hi, lane_pad), jnp.pad(w_r_lo, lane_pad)],
                                     axis=-1)

    return dict(
        w_router_split=w_router_split,
        ln1=p["ln1"], w_in_ext=w_in_ext, conv_w=p["conv_w"], conv_b=p["conv_b"],
        w_gate=[jnp.concatenate([p["lru_w_a"][r], p["lru_w_x"][r]], axis=-1).astype(BF16)
                for r in range(2)],
        lru_b_a=p["lru_b_a"], lru_b_x=p["lru_b_x"], lru_lam=p["lru_lam"],
        q_norm=p["q_norm"], kv_norm=p["kv_norm"], wq_all=wq_all,
        wkv=p["w_kv_up"].astype(BF16), gq=head_gain(p["q_head_norm"]),
        gk=head_gain(p["k_head_norm"]), w_out=p["w_out"].astype(BF16), ln2=p["ln2"],
        w1=p["w1"].astype(BF16), w3=p["w3"].astype(BF16),
        w2=p["w2"].astype(BF16), col_q=2 * d_lru, col_kv=2 * d_lru + q_lora, col_kr=s3)


def _rope_cs_table(s_len):
    pos = jnp.arange(s_len, dtype=F32)
    inv = ROPE_BASE ** (-jnp.arange(0, QK_ROPE, 2, dtype=F32) / QK_ROPE)
    ang = pos[:, None] * inv[None, :]
    cos, sin = jnp.cos(ang), jnp.sin(ang)
    return jnp.concatenate([cos, cos, -sin, sin], axis=-1)


def _layer(x, n_prompt_seq, p, cs_table):
    n_seq, s_len, d = x.shape
    rows = n_seq * s_len
    d_lru = p["conv_w"].shape[-1]

    h = norm_matmul(x.reshape(rows, d), p["ln1"], p["w_in_ext"], tm=_tile(rows, 256))
    h3 = h.reshape(n_seq, s_len, -1)

    t_len = _tile(s_len, 512)
    h_dir = [lru_scan(h3, p["conv_w"], p["conv_b"], p["w_gate"][r], p["lru_b_a"][r],
                      p["lru_b_x"][r], p["lru_lam"][r], t_len, reverse=bool(r))
             for r in range(2)]

    q, k, v = attn_prep(h3, cs_table, p["q_norm"], p["kv_norm"], p["wq_all"], p["wkv"],
                        p["gq"], p["gk"], _tile(s_len, 256), p["col_q"], p["col_kv"],
                        p["col_kr"])
    att = flash_attention(q, k, v, tq=_tile(s_len, 1024), tk=_tile(s_len, 1024))

    x1_lin, xn_lin, logits_t = mix_out(
        h_dir[0].reshape(rows, d_lru), h_dir[1].reshape(rows, d_lru), h,
        att.reshape(rows, -1), x.reshape(rows, d), p["w_out"], p["ln2"], p["w_router_split"],
        N_EXPERTS, tm=_tile(rows, 256))

    bounds = (0, n_prompt_seq * s_len, rows)
    caps = [max(1, (CAPACITY * (hi - lo)) // N_EXPERTS) for lo, hi in zip(bounds, bounds[1:])]
    tm = _tile(min(caps), 512)
    idx_l, gate_l, src_l, cnt_l = [], [], [], []
    slot_offset = 0
    for (lo, hi), cap in zip(zip(bounds, bounds[1:]), caps):
        lg = logits_t[:, lo:hi].reshape(N_EXPERTS, (hi - lo) // LANES, LANES)
        idx, gate, src, cnt, kmax = route(lg, cap, row_offset=lo, slot_offset=slot_offset)
        slot_offset += N_EXPERTS * cap
        idx_l.append(idx.reshape(-1, 1, tm))
        gate_l.append(gate.reshape(-1, tm, 1))
        src_l.append(src.transpose(1, 0, 2))
        cnt_l.append(jnp.stack([cnt, kmax], axis=1))
    first_tiles = N_EXPERTS * (caps[0] // tm)

    def tile_expert(t):
        return jnp.where(t < first_tiles, t // (caps[0] // tm),
                         (t - first_tiles) // (caps[1] // tm))

    ye_lin = expert_ffn_gather(jnp.concatenate(idx_l), jnp.concatenate(gate_l), xn_lin, d,
                               p["w1"], p["w3"], p["w2"], tile_expert, tf=_tile(d, 512))
    x2 = combine(jnp.concatenate(cnt_l), jnp.concatenate(src_l), x1_lin, ye_lin, d)
    return x2.reshape(n_seq, s_len, d)


def kernel(x_prompt, x_sample, ln1, w_in, conv_w, conv_b, lru_w_a, lru_b_a, lru_w_x, lru_b_x,
           lru_lam, q_norm, w_q_up, kv_norm, w_kv_up, q_head_norm, k_head_norm, w_out, ln2,
           w_router, w1, w3, w2, final_norm):
    depth = ln1.shape[0]
    n_prompt_seq = x_prompt.shape[0]
    x = jnp.concatenate([x_prompt, x_sample], axis=0)
    n_seq, s_len, d = x.shape
    cs_table = _rope_cs_table(s_len)
    for l in range(depth):
        p = _prep_layer_params(dict(
            ln1=ln1[l], w_in=w_in[l], conv_w=conv_w[l], conv_b=conv_b[l], lru_w_a=lru_w_a[l],
            lru_b_a=lru_b_a[l], lru_w_x=lru_w_x[l], lru_b_x=lru_b_x[l], lru_lam=lru_lam[l],
            q_norm=q_norm[l], w_q_up=w_q_up[l], kv_norm=kv_norm[l], w_kv_up=w_kv_up[l],
            q_head_norm=q_head_norm[l], k_head_norm=k_head_norm[l], w_out=w_out[l], ln2=ln2[l],
            w_router=w_router[l], w1=w1[l], w3=w3[l], w2=w2[l]))
        x = _layer(x, n_prompt_seq, p, cs_table)
    y = rms_norm_rows(x.reshape(n_seq * s_len, d), final_norm, _tile(n_seq * s_len, 512))
    y = y.reshape(n_seq, s_len, d)
    return y[:n_prompt_seq], y[n_prompt_seq:]
```

```python
import functools
import math

import jax
import jax.numpy as jnp
from jax import lax
from jax.experimental import pallas as pl
from jax.experimental.pallas import tpu as pltpu

F32 = jnp.float32
BF16 = jnp.bfloat16

EPS = 1e-6
LRU_BLOCKS = 8
CONV_WIDTH = 4
LRU_C = 8.0
V_HEAD_DIM = 128
QK_NOPE = 128
QK_ROPE = 64
QK_HEAD = QK_NOPE + QK_ROPE
ROPE_BASE = 10000.0
N_EXPERTS = 16
CAPACITY = 2

LANES = 128
SUBLANES = 8
HEAD_PAD = 2 * LANES
V7X_VMEM_LIMIT_BYTES = 56 * 1024 * 1024
LOG2E = math.log2(math.e)
COMBINE_STATIC_K = 3


def _tile(n, pref):
    t = min(n, pref)
    while n % t:
        t //= 2
    return t


def _params(*semantics):
    return pltpu.CompilerParams(dimension_semantics=semantics,
                                vmem_limit_bytes=V7X_VMEM_LIMIT_BYTES)


def _rms(x):
    return lax.rsqrt(jnp.mean(x * x, axis=-1, keepdims=True) + EPS)


def _store_row_linear(ref, base, x):
    n_rows, d = x.shape
    pitch = d // LANES
    for c in range(pitch):
        ref[pl.ds(base + c, n_rows, stride=pitch), :] = x[:, c * LANES:(c + 1) * LANES]


def _load_row_linear(ref, base, n_rows, pitch):
    return jnp.concatenate(
        [ref[pl.ds(base + c, n_rows, stride=pitch), :] for c in range(pitch)], axis=1)


def _norm_matmul_kernel(x_ref, g_ref, w_ref, o_ref):
    x = x_ref[...]
    y = x * _rms(x) * g_ref[...]
    o_ref[...] = jnp.dot(y.astype(BF16), w_ref[...], preferred_element_type=F32)


def norm_matmul(x, g, w_bf16, tm):
    rows, d = x.shape
    n = w_bf16.shape[1]
    return pl.pallas_call(
        _norm_matmul_kernel,
        out_shape=jax.ShapeDtypeStruct((rows, n), F32),
        grid=(rows // tm,),
        in_specs=[pl.BlockSpec((tm, d), lambda i: (i, 0)),
                  pl.BlockSpec((1, d), lambda i: (0, 0)),
                  pl.BlockSpec((d, n), lambda i: (0, 0))],
        out_specs=pl.BlockSpec((tm, n), lambda i: (i, 0)),
        compiler_params=_params("parallel"),
        name="norm_matmul",
    )(x, g.reshape(1, d), w_bf16)


def _norm_kernel(x_ref, g_ref, o_ref):
    x = x_ref[...]
    o_ref[...] = x * _rms(x) * g_ref[...]


def rms_norm_rows(x, g, tm):
    rows, d = x.shape
    return pl.pallas_call(
        _norm_kernel,
        out_shape=jax.ShapeDtypeStruct((rows, d), F32),
        grid=(rows // tm,),
        in_specs=[pl.BlockSpec((tm, d), lambda i: (i, 0)),
                  pl.BlockSpec((1, d), lambda i: (0, 0))],
        out_specs=pl.BlockSpec((tm, d), lambda i: (i, 0)),
        compiler_params=_params("parallel"),
        name="final_norm",
    )(x, g.reshape(1, d))


def _lru_kernel(x_ref, prev_ref, next_ref, cw_ref, cb_ref, wg_ref, ba_ref, bx_ref, lam_ref,
                o_ref, ext_sc, a_sc, u_sc, carry_sc, *, reverse):
    c = pl.program_id(1)
    n_c = pl.num_programs(1)
    t_len, ch = a_sc.shape
    blk = ch // LRU_BLOCKS
    first_in_time = (c == n_c - 1) if reverse else (c == 0)
    last_in_time = (c == 0) if reverse else (c == n_c - 1)

    @pl.when(c == 0)
    def _():
        carry_sc[...] = jnp.zeros(carry_sc.shape, F32)

    ext_sc[0:SUBLANES, :] = jnp.where(first_in_time, 0.0, prev_ref[0])
    ext_sc[SUBLANES:SUBLANES + t_len, :] = x_ref[0]
    ext_sc[SUBLANES + t_len:, :] = jnp.where(last_in_time, 0.0, next_ref[0])
    xc = cb_ref[...]
    for tap in range(CONV_WIDTH):
        off = SUBLANES + tap - 2
        xc = xc + cw_ref[tap:tap + 1, :] * ext_sc[off:off + t_len, :]

    lam = lam_ref[...]
    log_sig_lam = jnp.minimum(lam, 0.0) - jnp.log1p(jnp.exp(-jnp.abs(lam)))
    for n in range(LRU_BLOCKS):
        sl = slice(n * blk, (n + 1) * blk)
        xcn = xc[:, sl]
        g = jnp.dot(xcn.astype(BF16), wg_ref[n], preferred_element_type=F32)
        r = jax.nn.sigmoid(g[:, :blk] + ba_ref[:, sl])
        i = jax.nn.sigmoid(g[:, blk:] + bx_ref[:, sl])
        log_a = LRU_C * r * log_sig_lam[:, sl]
        a_sc[:, sl] = jnp.exp(log_a)
        u_sc[:, sl] = jnp.sqrt(1.0 - jnp.exp(2.0 * log_a)) * (i * xcn)

    row = lax.broadcasted_iota(jnp.int32, (SUBLANES, ch), 0)
    n_groups = t_len // SUBLANES

    def group(gi, carry):
        g0 = (n_groups - 1 - gi) if reverse else gi
        start = pl.multiple_of(g0 * SUBLANES, SUBLANES)
        a = a_sc[pl.ds(start, SUBLANES), :]
        u = u_sc[pl.ds(start, SUBLANES), :]
        for d in (1, 2, 4):
            shift = SUBLANES - d if reverse else d
            a_s = pltpu.roll(a, shift, 0)
            u_s = pltpu.roll(u, shift, 0)
            valid = (row < SUBLANES - d) if reverse else (row >= d)
            u = jnp.where(valid, a * u_s + u, u)
            a = jnp.where(valid, a * a_s, a)
        h = u + a * carry
        o_ref[0, pl.ds(start, SUBLANES), :] = h
        edge = h[0:1, :] if reverse else h[SUBLANES - 1:SUBLANES, :]
        return jnp.broadcast_to(edge, (SUBLANES, ch))

    carry_sc[...] = lax.fori_loop(0, n_groups, group, carry_sc[...], unroll=2)


def lru_scan(h3, conv_w, conv_b, w_gate, b_a, b_x, lam, t_len, reverse):
    n_seq, s_len, _ = h3.shape
    ch = conv_w.shape[-1]
    n_c = s_len // t_len
    per = t_len // SUBLANES
    n_sub = s_len // SUBLANES
    cidx = (lambda c: n_c - 1 - c) if reverse else (lambda c: c)
    return pl.pallas_call(
        functools.partial(_lru_kernel, reverse=reverse),
        out_shape=jax.ShapeDtypeStruct((n_seq, s_len, ch), F32),
        grid=(n_seq, n_c),
        in_specs=[
            pl.BlockSpec((1, t_len, ch), lambda s, c: (s, cidx(c), 0)),
            pl.BlockSpec((1, SUBLANES, ch),
                         lambda s, c: (s, jnp.maximum(cidx(c) * per - 1, 0), 0)),
            pl.BlockSpec((1, SUBLANES, ch),
                         lambda s, c: (s, jnp.minimum((cidx(c) + 1) * per, n_sub - 1), 0)),
            pl.BlockSpec((CONV_WIDTH, ch), lambda s, c: (0, 0)),
            pl.BlockSpec((1, ch), lambda s, c: (0, 0)),
            pl.BlockSpec(w_gate.shape, lambda s, c: (0, 0, 0)),
            pl.BlockSpec((1, ch), lambda s, c: (0, 0)),
            pl.BlockSpec((1, ch), lambda s, c: (0, 0)),
            pl.BlockSpec((1, ch), lambda s, c: (0, 0)),
        ],
        out_specs=pl.BlockSpec((1, t_len, ch), lambda s, c: (s, cidx(c), 0)),
        scratch_shapes=[pltpu.VMEM((t_len + 2 * SUBLANES, ch), F32),
                        pltpu.VMEM((t_len, ch), F32),
                        pltpu.VMEM((t_len, ch), F32),
                        pltpu.VMEM((SUBLANES, ch), F32)],
        compiler_params=_params("parallel", "arbitrary"),
        name="lru_rev" if reverse else "lru_fwd",
    )(h3, h3, h3, conv_w, conv_b.reshape(1, ch), w_gate, b_a.reshape(1, ch),
      b_x.reshape(1, ch), lam.reshape(1, ch))


def _attn_prep_kernel(cq_ref, ckv_ref, kr_ref, cs_ref, qn_ref, kvn_ref, wq_ref, wkv_ref,
                      gq_ref, gk_ref, q_out, k_out, v_out, *, n_heads, q_scale):
    cq = cq_ref[...]
    ckv = ckv_ref[...]
    qa = jnp.dot((cq * _rms(cq) * qn_ref[...]).astype(BF16), wq_ref[...],
                 preferred_element_type=F32)
    kva = jnp.dot((ckv * _rms(ckv) * kvn_ref[...]).astype(BF16), wkv_ref[...],
                  preferred_element_type=F32)
    cs = cs_ref[...]
    lane = lax.broadcasted_iota(jnp.int32, cs.shape, 1)
    rope_lanes = lane < QK_ROPE
    gq = gq_ref[...]
    gk = gk_ref[...]
    vk = kr_ref[...]
    k_rope_ss = 0.5 * jnp.sum(vk * vk, axis=-1, keepdims=True)
    bk = vk * gk[:, LANES:] * cs
    bk = bk + pltpu.roll(bk, QK_ROPE, 1)
    ones = jnp.ones(cs.shape, BF16)
    for h in range(n_heads):
        base = h * HEAD_PAD
        qn = qa[:, base:base + LANES]
        qv = qa[:, base + LANES:base + HEAD_PAD]
        ss = jnp.sum(qn * qn, axis=-1, keepdims=True) + 0.5 * jnp.sum(qv * qv, axis=-1,
                                                                     keepdims=True)
        rs = lax.rsqrt(ss * (1.0 / QK_HEAD) + EPS) * q_scale
        bq = qv * gq[:, LANES:] * cs
        bq = bq + pltpu.roll(bq, QK_ROPE, 1)
        q_out[0, h, :, 0:LANES] = (qn * gq[:, :LANES] * rs).astype(BF16)
        q_out[0, h, :, LANES:] = (bq * rs).astype(BF16)

        kn = kva[:, base:base + LANES]
        ss = jnp.sum(kn * kn, axis=-1, keepdims=True) + k_rope_ss
        rs = lax.rsqrt(ss * (1.0 / QK_HEAD) + EPS)
        k_out[0, h, :, 0:LANES] = (kn * gk[:, :LANES] * rs).astype(BF16)
        k_out[0, h, :, LANES:] = jnp.where(rope_lanes, bk * rs, 0.0).astype(BF16)

        v_out[0, h, :, 0:LANES] = kva[:, base + LANES:base + HEAD_PAD].astype(BF16)
        v_out[0, h, :, LANES:] = ones


def attn_prep(h3, cs_table, q_norm, kv_norm, wq_all, wkv, gq, gk, tm, col_q, col_kv, col_kr):
    n_seq, s_len, _ = h3.shape
    q_lora = q_norm.shape[-1]
    kv_lora = kv_norm.shape[-1]
    n_heads = wq_all.shape[-1] // HEAD_PAD
    q_scale = QK_HEAD ** -0.5 * LOG2E
    out = jax.ShapeDtypeStruct((n_seq, n_heads, s_len, HEAD_PAD), BF16)
    head_spec = pl.BlockSpec((1, n_heads, tm, HEAD_PAD), lambda s, i: (s, 0, i, 0))
    const2 = lambda s, i: (0, 0)
    return pl.pallas_call(
        functools.partial(_attn_prep_kernel, n_heads=n_heads, q_scale=q_scale),
        out_shape=(out, out, out),
        grid=(n_seq, s_len // tm),
        in_specs=[
            pl.BlockSpec((None, tm, q_lora), lambda s, i: (s, i, col_q // q_lora)),
            pl.BlockSpec((None, tm, kv_lora), lambda s, i: (s, i, col_kv // kv_lora)),
            pl.BlockSpec((None, tm, LANES), lambda s, i: (s, i, col_kr // LANES)),
            pl.BlockSpec((tm, LANES), lambda s, i: (i, 0)),
            pl.BlockSpec((1, q_lora), const2),
            pl.BlockSpec((1, kv_lora), const2),
            pl.BlockSpec(wq_all.shape, const2),
            pl.BlockSpec(wkv.shape, const2),
            pl.BlockSpec((1, HEAD_PAD), const2),
            pl.BlockSpec((1, HEAD_PAD), const2),
        ],
        out_specs=(head_spec, head_spec, head_spec),
        compiler_params=_params("parallel", "parallel"),
        name="attn_prep",
    )(h3, h3, h3, cs_table, q_norm.reshape(1, -1), kv_norm.reshape(1, -1), wq_all, wkv, gq, gk)


def _flash_kernel(q_ref, k_ref, v_ref, o_ref, s_a, s_b, p_a, p_b, al_a, al_b, m_sc, acc_sc,
                  *, tk, n_kv):
    q = q_ref[0, 0]
    m_sc[...] = jnp.full(m_sc.shape, -jnp.inf, F32)
    acc_sc[...] = jnp.zeros(acc_sc.shape, F32)

    def scores(j, s_ref):
        k = k_ref[0, 0, j * tk:(j + 1) * tk, :]
        s_ref[...] = lax.dot_general(q, k, (((1,), (1,)), ((), ())),
                                     preferred_element_type=F32)

    def softmax(s_ref, p_ref, al_ref):
        s = s_ref[...]
        m_prev = m_sc[...]
        m_new = jnp.maximum(m_prev, jnp.max(s, axis=-1, keepdims=True))
        al_ref[...] = jnp.exp2(m_prev - m_new)
        p_ref[...] = jnp.exp2(s - m_new[:, :1]).astype(BF16)
        m_sc[...] = m_new

    def values(j, p_ref, al_ref):
        v = v_ref[0, 0, j * tk:(j + 1) * tk, :]
        pv = jnp.dot(p_ref[...], v, preferred_element_type=F32)
        alpha = al_ref[...]
        acc_sc[:, :LANES] = alpha * acc_sc[:, :LANES] + pv[:, :LANES]
        acc_sc[:, LANES:] = alpha * acc_sc[:, LANES:] + pv[:, LANES:]

    bufs = ((s_a, p_a, al_a), (s_b, p_b, al_b))
    scores(0, s_a)
    for j in range(n_kv):
        s_cur, p_cur, al_cur = bufs[j % 2]
        s_nxt, p_prev, al_prev = bufs[(j + 1) % 2]
        if j + 1 < n_kv:
            scores(j + 1, s_nxt)
        softmax(s_cur, p_cur, al_cur)
        if j >= 1:
            values(j - 1, p_prev, al_prev)
    _, p_last, al_last = bufs[(n_kv - 1) % 2]
    values(n_kv - 1, p_last, al_last)
    o_ref[0] = (acc_sc[:, :LANES] / acc_sc[:, LANES:]).astype(o_ref.dtype)


def flash_attention(q, k, v, tq, tk):
    n_seq, n_heads, s_len, dq = q.shape
    n_kv = s_len // tk
    kv_spec = pl.BlockSpec((1, 1, s_len, dq), lambda s, h, i: (s, h, 0, 0))
    return pl.pallas_call(
        functools.partial(_flash_kernel, tk=tk, n_kv=n_kv),
        out_shape=jax.ShapeDtypeStruct((n_seq, s_len, n_heads * V_HEAD_DIM), BF16),
        grid=(n_seq, n_heads, s_len // tq),
        in_specs=[pl.BlockSpec((1, 1, tq, dq), lambda s, h, i: (s, h, i, 0)), kv_spec, kv_spec],
        out_specs=pl.BlockSpec((1, tq, V_HEAD_DIM), lambda s, h, i: (s, i, h)),
        scratch_shapes=[pltpu.VMEM((tq, tk), F32), pltpu.VMEM((tq, tk), F32),
                        pltpu.VMEM((tq, tk), BF16), pltpu.VMEM((tq, tk), BF16),
                        pltpu.VMEM((tq, LANES), F32), pltpu.VMEM((tq, LANES), F32),
                        pltpu.VMEM((tq, LANES), F32), pltpu.VMEM((tq, dq), F32)],
        compiler_params=_params("parallel", "parallel", "arbitrary"),
        name="flash_attention",
    )(q, k, v)


def _mix_out_kernel(hf_ref, hb_ref, y_ref, att_ref, x_ref, w_ref, g_ref, wr_ref,
                    x1_ref, xn_ref, lg_ref):
    ch = hf_ref.shape[-1]
    y = y_ref[...]
    gelu = 0.5 * y * (1.0 + jnp.tanh(math.sqrt(2.0 / math.pi) * (y + 0.044715 * (y * y * y))))
    rnn = ((hf_ref[...] + hb_ref[...]) * gelu).astype(BF16)
    x1 = (x_ref[...]
          + jnp.dot(rnn, w_ref[0:ch, :], preferred_element_type=F32)
          + jnp.dot(att_ref[...], w_ref[ch:, :], preferred_element_type=F32))
    xn = x1 * _rms(x1) * g_ref[...]
    _store_row_linear(x1_ref, 0, x1)
    _store_row_linear(xn_ref, 0, xn)
    wr = wr_ref[...]
    x_hi = xn.astype(BF16)
    x_lo = (xn - x_hi.astype(F32)).astype(BF16)
    both = jnp.dot(x_hi, wr, preferred_element_type=F32)
    lo = jnp.dot(x_lo, wr[:, :LANES], preferred_element_type=F32)
    lg = both[:, :LANES] + both[:, LANES:] + lo
    lg_ref[...] = jnp.transpose(lg)[0:lg_ref.shape[0], :]


def mix_out(hf, hb, h2, att, x, w_out, ln2, w_router_split, n_e, tm):
    rows, ch = hf.shape
    d = x.shape[-1]
    pitch = d // LANES
    row_spec = lambda width: pl.BlockSpec((tm, width), lambda i: (i, 0))
    lin_spec = pl.BlockSpec((tm * pitch, LANES), lambda i: (i, 0))
    lin_shape = jax.ShapeDtypeStruct((rows * pitch, LANES), F32)
    return pl.pallas_call(
        _mix_out_kernel,
        out_shape=(lin_shape, lin_shape, jax.ShapeDtypeStruct((n_e, rows), F32)),
        grid=(rows // tm,),
        in_specs=[row_spec(ch), row_spec(ch),
                  pl.BlockSpec((tm, ch), lambda i: (i, 1)),
                  row_spec(att.shape[-1]), row_spec(d),
                  pl.BlockSpec(w_out.shape, lambda i: (0, 0)),
                  pl.BlockSpec((1, d), lambda i: (0, 0)),
                  pl.BlockSpec(w_router_split.shape, lambda i: (0, 0))],
        out_specs=(lin_spec, lin_spec, pl.BlockSpec((n_e, tm), lambda i: (0, i))),
        compiler_params=_params("parallel"),
        name="mix_out",
    )(hf, hb, h2, att, x, w_out, ln2.reshape(1, d), w_router_split)


def _sum_all(x):
    return jnp.sum(jnp.sum(x, axis=0, keepdims=True), axis=1, keepdims=True)


def _route_kernel(lg_ref, idx_ref, gate_ref, src_ref, cnt_ref, kmax_ref, *, cap, row_offset,
                  slot_offset, s_chunk):
    e = pl.program_id(0)
    n_e, nb, _ = lg_ref.shape
    lg = lg_ref[...]
    m = jnp.max(lg, axis=0)
    den = jnp.sum(jnp.exp(lg - m[None]), axis=0)
    aff = jnp.exp(lg_ref[e] - m) / den
    bits = pltpu.bitcast(aff, jnp.int32)

    def bit_step(i, prefix):
        cand = prefix | jnp.left_shift(jnp.int32(1), 30 - i)
        n_ge = _sum_all(jnp.where(bits >= cand, 1.0, 0.0))
        return jnp.where(n_ge >= cap, cand, prefix)

    thr = lax.fori_loop(0, 31, bit_step, jnp.zeros((1, 1), jnp.int32))

    r128 = lax.broadcasted_iota(jnp.int32, (LANES, LANES), 0)
    c128 = lax.broadcasted_iota(jnp.int32, (LANES, LANES), 1)
    upper_lanes = jnp.where(r128 <= c128, 1.0, 0.0).astype(BF16)
    rb = lax.broadcasted_iota(jnp.int32, (nb, nb), 0)
    cb = lax.broadcasted_iota(jnp.int32, (nb, nb), 1)
    lower_rows = jnp.where(cb < rb, 1.0, 0.0).astype(BF16)
    upper_rows = jnp.where(rb <= cb, 1.0, 0.0).astype(BF16)

    def cumsum_tokens(mf):
        row_cum = jnp.dot(mf.astype(BF16), upper_lanes, preferred_element_type=F32)
        tot = jnp.broadcast_to(row_cum[:, LANES - 1:LANES], (nb, LANES))
        return row_cum + jnp.dot(lower_rows, tot.astype(BF16), preferred_element_type=F32)

    gt = bits > thr
    eq = bits == thr
    eq_f = jnp.where(eq, 1.0, 0.0)
    need = cap - _sum_all(jnp.where(gt, 1.0, 0.0))
    eq_rank = cumsum_tokens(eq_f) - eq_f
    sel = gt | (eq & (eq_rank < need))
    sel_f = jnp.where(sel, 1.0, 0.0)
    cum = cumsum_tokens(sel_f)

    @pl.when(e == 0)
    def _():
        cnt_ref[...] = jnp.zeros(cnt_ref.shape, jnp.int32)
        src_ref[...] = jnp.zeros(src_ref.shape, jnp.int32)

    rank = cnt_ref[...]
    flat = (cum - sel_f).astype(jnp.int32) + (slot_offset + e * cap)
    for k in range(n_e):
        src_ref[k] = jnp.where(sel & (rank == k), flat, src_ref[k])
    cnt_new = rank + sel.astype(jnp.int32)
    cnt_ref[...] = cnt_new
    row_max = jnp.max(cnt_new.astype(F32), axis=1, keepdims=True)
    kmax_ref[...] = jnp.broadcast_to(row_max, kmax_ref.shape).astype(jnp.int32)

    tot_lane = lax.dot_general(jnp.ones((SUBLANES, LANES), BF16), sel_f.astype(BF16),
                               (((1,), (1,)), ((), ())), preferred_element_type=F32)
    end_incl = jnp.dot(tot_lane.astype(BF16), upper_rows, preferred_element_type=F32)[0:1]
    end_excl = end_incl - tot_lane[0:1]
    cum_hi = jnp.floor(cum * (1.0 / 64.0))
    a1 = aff.astype(BF16)
    r1 = aff - a1.astype(F32)
    a2 = r1.astype(BF16)
    a3 = (r1 - a2.astype(F32)).astype(BF16)
    payload = jnp.concatenate(
        [cum_hi.astype(BF16), (cum - 64.0 * cum_hi).astype(BF16), a1, a2, a3], axis=1)
    row_id = lax.broadcasted_iota(jnp.int32, (s_chunk, nb), 1).astype(F32)
    lane_id = lax.broadcasted_iota(jnp.int32, (s_chunk, LANES), 1)

    def chunk(c, carry):
        base = pl.multiple_of(c * s_chunk, s_chunk)
        s_col = (lax.broadcasted_iota(jnp.int32, (s_chunk, 1), 0) + base).astype(F32)
        hit = (end_excl <= s_col) & (s_col < end_incl)
        hit_f = jnp.where(hit, 1.0, 0.0)
        got = jnp.dot(hit_f.astype(BF16), payload, preferred_element_type=F32)
        cum_row = 64.0 * got[:, 0:LANES] + got[:, LANES:2 * LANES]
        aff_row = got[:, 2 * LANES:3 * LANES] + got[:, 3 * LANES:4 * LANES] + got[:, 4 * LANES:]
        j = jnp.sum(jnp.where(cum_row <= s_col, 1.0, 0.0), axis=1, keepdims=True)
        blk = jnp.sum(hit_f * row_id, axis=1, keepdims=True)
        idx_ref[0, pl.ds(base, s_chunk), :] = (blk * LANES + j).astype(jnp.int32) + row_offset
        gate_ref[0, pl.ds(base, s_chunk), :] = jnp.sum(
            jnp.where(lane_id == j.astype(jnp.int32), aff_row, 0.0), axis=1, keepdims=True)
        return carry

    lax.fori_loop(0, cap // s_chunk, chunk, 0)


def route(lg, cap, row_offset, slot_offset):
    n_e, nb, _ = lg.shape
    assert cap <= 64 * 256, "slot counts are split into two bf16-exact digits"
    s_chunk = _tile(cap, 512)
    return pl.pallas_call(
        functools.partial(_route_kernel, cap=cap, row_offset=row_offset,
                          slot_offset=slot_offset, s_chunk=s_chunk),
        out_shape=(jax.ShapeDtypeStruct((n_e, cap, 1), jnp.int32),
                   jax.ShapeDtypeStruct((n_e, cap, 1), F32),
                   jax.ShapeDtypeStruct((n_e, nb, LANES), jnp.int32),
                   jax.ShapeDtypeStruct((nb, LANES), jnp.int32),
                   jax.ShapeDtypeStruct((nb, LANES), jnp.int32)),
        grid=(n_e,),
        in_specs=[pl.BlockSpec((n_e, nb, LANES), lambda e: (0, 0, 0))],
        out_specs=(pl.BlockSpec((1, cap, 1), lambda e: (e, 0, 0)),
                   pl.BlockSpec((1, cap, 1), lambda e: (e, 0, 0)),
                   pl.BlockSpec((n_e, nb, LANES), lambda e: (0, 0, 0)),
                   pl.BlockSpec((nb, LANES), lambda e: (0, 0)),
                   pl.BlockSpec((nb, LANES), lambda e: (0, 0))),
        compiler_params=_params("arbitrary"),
        name="route",
    )(lg)


def _ffn_gather_kernel(idx_ref, idx_next_ref, gate_ref, xn_hbm, w1_ref, w3_ref, w2_ref,
                       o_ref, xbuf, xs, acc, sem, *, n_f):
    t = pl.program_id(0)
    f = pl.program_id(1)
    n_t = pl.num_programs(0)
    tm, d = xs.shape
    pitch = d // LANES
    slot = t % 2
    rows_per_step = tm // n_f

    def row_copy(idx_smem, r, slot_):
        src = pl.multiple_of(idx_smem[0, 0, r] * pitch, pitch)
        dst = pl.multiple_of((slot_ * tm + r) * pitch, pitch)
        return pltpu.make_async_copy(xn_hbm.at[pl.ds(src, pitch), :],
                                     xbuf.at[pl.ds(dst, pitch), :], sem.at[slot_])

    def start_rows(idx_smem, slot_):
        def body(r, carry):
            row_copy(idx_smem, r, slot_).start()
            return carry
        lax.fori_loop(0, tm, body, 0, unroll=8)

    def wait_slot(slot_):
        base = pl.multiple_of(slot_ * (tm * pitch), SUBLANES)
        pltpu.make_async_copy(xn_hbm.at[pl.ds(0, tm * pitch), :],
                              xbuf.at[pl.ds(base, tm * pitch), :], sem.at[slot_]).wait()

    @pl.when((t == 0) & (f == 0))
    def _():
        start_rows(idx_ref, 0)

    for r in range(rows_per_step):
        row_copy(idx_next_ref, f * rows_per_step + r, 1 - slot).start()

    @pl.when(f == 0)
    def _():
        wait_slot(slot)
        base = pl.multiple_of(slot * (tm * pitch), SUBLANES)
        xs[...] = _load_row_linear(xbuf, base, tm, pitch).astype(BF16)

    x = xs[...]
    a = jnp.dot(x, w1_ref[0], preferred_element_type=F32)
    b = jnp.dot(x, w3_ref[0], preferred_element_type=F32)
    hid = (a * jax.nn.sigmoid(a)) * b
    y = jnp.dot(hid.astype(BF16), w2_ref[0], preferred_element_type=F32)

    @pl.when(f == 0)
    def _():
        acc[...] = y

    @pl.when(f > 0)
    def _():
        acc[...] += y

    @pl.when(f == n_f - 1)
    def _():
        _store_row_linear(o_ref, 0, acc[...] * gate_ref[0])

        @pl.when(t == n_t - 1)
        def _():
            wait_slot(1 - slot)


def expert_ffn_gather(idx_tiles, gate_tiles, xn_lin, d, w1, w3, w2, tile_expert, tf):
    n_tiles, _, tm = idx_tiles.shape
    pitch = d // LANES
    d_ff = w1.shape[-1]
    n_f = d_ff // tf
    assert tm % n_f == 0
    smem = pltpu.MemorySpace.SMEM
    return pl.pallas_call(
        functools.partial(_ffn_gather_kernel, n_f=n_f),
        out_shape=jax.ShapeDtypeStruct((n_tiles * tm * pitch, LANES), F32),
        grid=(n_tiles, d_ff // tf),
        in_specs=[pl.BlockSpec((1, 1, tm), lambda t, f: (t, 0, 0), memory_space=smem),
                  pl.BlockSpec((1, 1, tm), lambda t, f: (jnp.minimum(t + 1, n_tiles - 1), 0, 0),
                               memory_space=smem),
                  pl.BlockSpec((1, tm, 1), lambda t, f: (t, 0, 0)),
                  pl.BlockSpec(memory_space=pl.ANY),
                  pl.BlockSpec((1, d, tf), lambda t, f: (tile_expert(t), 0, f)),
                  pl.BlockSpec((1, d, tf), lambda t, f: (tile_expert(t), 0, f)),
                  pl.BlockSpec((1, tf, d), lambda t, f: (tile_expert(t), f, 0))],
        out_specs=pl.BlockSpec((tm * pitch, LANES), lambda t, f: (t, 0)),
        scratch_shapes=[pltpu.VMEM((2 * tm * pitch, LANES), F32),
                        pltpu.VMEM((tm, d), BF16),
                        pltpu.VMEM((tm, d), F32),
                        pltpu.SemaphoreType.DMA((2,))],
        compiler_params=_params("arbitrary", "arbitrary"),
        name="expert_ffn",
    )(idx_tiles, idx_tiles, gate_tiles, xn_lin, w1, w3, w2)


def _combine_kernel(cnt_ref, src_ref, cnt_next_ref, src_next_ref, x1_ref, ye_hbm, o_ref,
                    buf, acc, sem, meta):
    i = pl.program_id(0)
    n_i = pl.num_programs(0)
    slot = i % 2
    n_tok, d = o_ref.shape
    pitch = d // LANES
    n_e = src_ref.shape[1]
    slab = n_tok * pitch

    def row_copy(row, k, t, slot_):
        src = pl.multiple_of(row * pitch, pitch)
        dst = pl.multiple_of(((slot_ * n_e + k) * n_tok + t) * pitch, pitch)
        return pltpu.make_async_copy(ye_hbm.at[pl.ds(src, pitch), :],
                                     buf.at[pl.ds(dst, pitch), :], sem.at[slot_])

    def slab_start(k, slot_):
        return pl.multiple_of((slot_ * n_e + k) * slab, SUBLANES)

    def fetch(cnt_smem, src_smem, slot_):
        k_max = cnt_smem[0, 1, 0]

        def clear(k, carry):
            buf[pl.ds(slab_start(k, slot_), slab), :] = jnp.zeros((slab, LANES), F32)
            return carry
        lax.fori_loop(0, k_max, clear, 0)

        def token(t, total):
            c = cnt_smem[0, 0, t]

            def one(k, carry):
                row_copy(src_smem[0, k, t], k, t, slot_).start()
                return carry

            for k in range(COMBINE_STATIC_K):
                @pl.when(c > k)
                def _():
                    one(k, 0)

            @pl.when(c > COMBINE_STATIC_K)
            def _():
                lax.fori_loop(COMBINE_STATIC_K, c, one, 0)
            return total + c
        meta[slot_, 0] = lax.fori_loop(0, n_tok, token, 0)
        meta[slot_, 1] = k_max

    @pl.when(i == 0)
    def _():
        fetch(cnt_ref, src_ref, 0)

    @pl.when(i + 1 < n_i)
    def _():
        fetch(cnt_next_ref, src_next_ref, 1 - slot)

    total = meta[slot, 0]
    for bit in range((n_e * n_tok).bit_length()):
        @pl.when(((total >> bit) & 1) == 1)
        def _():
            n_rows = (1 << bit) * pitch
            pltpu.make_async_copy(ye_hbm.at[pl.ds(0, n_rows), :],
                                  buf.at[pl.ds(slab_start(0, slot), n_rows), :],
                                  sem.at[slot]).wait()

    acc[...] = x1_ref[...]

    def add(k, carry):
        acc[...] += buf[pl.ds(slab_start(k, slot), slab), :]
        return carry
    lax.fori_loop(0, meta[slot, 1], add, 0)
    o_ref[...] = _load_row_linear(acc, 0, n_tok, pitch)


def combine(cnt_blocks, src_blocks, x1_lin, ye_lin, d):
    n_blk, n_e, n_tok = src_blocks.shape
    pitch = d // LANES
    smem = pltpu.MemorySpace.SMEM
    nxt = lambda i: jnp.minimum(i + 1, n_blk - 1)
    return pl.pallas_call(
        _combine_kernel,
        out_shape=jax.ShapeDtypeStruct((n_blk * n_tok, d), F32),
        grid=(n_blk,),
        in_specs=[pl.BlockSpec((1, 2, n_tok), lambda i: (i, 0, 0), memory_space=smem),
                  pl.BlockSpec((1, n_e, n_tok), lambda i: (i, 0, 0), memory_space=smem),
                  pl.BlockSpec((1, 2, n_tok), lambda i: (nxt(i), 0, 0), memory_space=smem),
                  pl.BlockSpec((1, n_e, n_tok), lambda i: (nxt(i), 0, 0), memory_space=smem),
                  pl.BlockSpec((n_tok * pitch, LANES), lambda i: (i, 0)),
                  pl.BlockSpec(memory_space=pl.ANY)],
        out_specs=pl.BlockSpec((n_tok, d), lambda i: (i, 0)),
        scratch_shapes=[pltpu.VMEM((2 * n_e * n_tok * pitch, LANES), F32),
                        pltpu.VMEM((n_tok * pitch, LANES), F32),
                        pltpu.SemaphoreType.DMA((2,)),
                        pltpu.SMEM((2, 2), jnp.int32)],
        compiler_params=_params("arbitrary"),
        name="combine",
    )(cnt_blocks, src_blocks, cnt_blocks, src_blocks, x1_lin, ye_lin)


def _swap_halves(w):
    half = w.shape[-1] // 2
    return jnp.concatenate([w[..., half:], w[..., :half]], axis=-1)


def _prep_layer_params(p):
    d_lru = p["conv_w"].shape[-1]
    q_lora = p["q_norm"].shape[-1]
    kv_lora = p["kv_norm"].shape[-1]
    n_heads = p["w_q_up"].shape[-1] // QK_HEAD
    s3 = 2 * d_lru + q_lora + kv_lora
    w_in = p["w_in"]
    w_in_ext = jnp.concatenate([w_in, _swap_halves(w_in[:, s3:])], axis=-1).astype(BF16)
    wq = p["w_q_up"].reshape(q_lora, n_heads, QK_HEAD)
    wq_rope = wq[..., QK_NOPE:]
    wq_all = jnp.concatenate([wq[..., :QK_NOPE], wq_rope, _swap_halves(wq_rope)], axis=-1)
    wq_all = wq_all.reshape(q_lora, n_heads * HEAD_PAD).astype(BF16)

    def head_gain(g):
        g_rope = g[QK_NOPE:]
        return jnp.concatenate([g[:QK_NOPE], g_rope, _swap_halves(g_rope)]).reshape(1, HEAD_PAD)

    w_r = p["w_router"]
    w_r_hi = w_r.astype(BF16)
    w_r_lo = (w_r - w_r_hi.astype(F32)).astype(BF16)
    lane_pad = ((0, 0), (0, LANES - w_r.shape[-1]))
    w_router_split = jnp.concatenate([jnp.pad(w_r_hi, lane_pad), jnp.pad(w_r_lo, lane_pad)],
                                     axis=-1)

    return dict(
        w_router_split=w_router_split,
        ln1=p["ln1"], w_in_ext=w_in_ext, conv_w=p["conv_w"], conv_b=p["conv_b"],
        w_gate=[jnp.concatenate([p["lru_w_a"][r], p["lru_w_x"][r]], axis=-1).astype(BF16)
                for r in range(2)],
        lru_b_a=p["lru_b_a"], lru_b_x=p["lru_b_x"], lru_lam=p["lru_lam"],
        q_norm=p["q_norm"], kv_norm=p["kv_norm"], wq_all=wq_all,
        wkv=p["w_kv_up"].astype(BF16), gq=head_gain(p["q_head_norm"]),
        gk=head_gain(p["k_head_norm"]), w_out=p["w_out"].astype(BF16), ln2=p["ln2"],
        w1=p["w1"].astype(BF16), w3=p["w3"].astype(BF16),
        w2=p["w2"].astype(BF16), col_q=2 * d_lru, col_kv=2 * d_lru + q_lora, col_kr=s3)


def _rope_cs_table(s_len):
    pos = jnp.arange(s_len, dtype=F32)
    inv = ROPE_BASE ** (-jnp.arange(0, QK_ROPE, 2, dtype=F32) / QK_ROPE)
    ang = pos[:, None] * inv[None, :]
    cos, sin = jnp.cos(ang), jnp.sin(ang)
    return jnp.concatenate([cos, cos, -sin, sin], axis=-1)


def _layer(x, n_prompt_seq, p, cs_table):
    n_seq, s_len, d = x.shape
    rows = n_seq * s_len
    d_lru = p["conv_w"].shape[-1]

    h = norm_matmul(x.reshape(rows, d), p["ln1"], p["w_in_ext"], tm=_tile(rows, 256))
    h3 = h.reshape(n_seq, s_len, -1)

    t_len = _tile(s_len, 512)
    h_dir = [lru_scan(h3, p["conv_w"], p["conv_b"], p["w_gate"][r], p["lru_b_a"][r],
                      p["lru_b_x"][r], p["lru_lam"][r], t_len, reverse=bool(r))
             for r in range(2)]

    q, k, v = attn_prep(h3, cs_table, p["q_norm"], p["kv_norm"], p["wq_all"], p["wkv"],
                        p["gq"], p["gk"], _tile(s_len, 512), p["col_q"], p["col_kv"],
                        p["col_kr"])
    att = flash_attention(q, k, v, tq=_tile(s_len, 1024), tk=_tile(s_len, 1024))

    x1_lin, xn_lin, logits_t = mix_out(
        h_dir[0].reshape(rows, d_lru), h_dir[1].reshape(rows, d_lru), h,
        att.reshape(rows, -1), x.reshape(rows, d), p["w_out"], p["ln2"], p["w_router_split"],
        N_EXPERTS, tm=_tile(rows, 256))

    bounds = (0, n_prompt_seq * s_len, rows)
    caps = [max(1, (CAPACITY * (hi - lo)) // N_EXPERTS) for lo, hi in zip(bounds, bounds[1:])]
    tm = _tile(min(caps), 512)
    idx_l, gate_l, src_l, cnt_l = [], [], [], []
    slot_offset = 0
    for (lo, hi), cap in zip(zip(bounds, bounds[1:]), caps):
        lg = logits_t[:, lo:hi].reshape(N_EXPERTS, (hi - lo) // LANES, LANES)
        idx, gate, src, cnt, kmax = route(lg, cap, row_offset=lo, slot_offset=slot_offset)
        slot_offset += N_EXPERTS * cap
        idx_l.append(idx.reshape(-1, 1, tm))
        gate_l.append(gate.reshape(-1, tm, 1))
        src_l.append(src.transpose(1, 0, 2))
        cnt_l.append(jnp.stack([cnt, kmax], axis=1))
    first_tiles = N_EXPERTS * (caps[0] // tm)

    def tile_expert(t):
        return jnp.where(t < first_tiles, t // (caps[0] // tm),
                         (t - first_tiles) // (caps[1] // tm))

    ye_lin = expert_ffn_gather(jnp.concatenate(idx_l), jnp.concatenate(gate_l), xn_lin, d,
                               p["w1"], p["w3"], p["w2"], tile_expert, tf=_tile(d, 1024))
    x2 = combine(jnp.concatenate(cnt_l), jnp.concatenate(src_l), x1_lin, ye_lin, d)
    return x2.reshape(n_seq, s_len, d)


def kernel(x_prompt, x_sample, ln1, w_in, conv_w, conv_b, lru_w_a, lru_b_a, lru_w_x, lru_b_x,
           lru_lam, q_norm, w_q_up, kv_norm, w_kv_up, q_head_norm, k_head_norm, w_out, ln2,
           w_router, w1, w3, w2, final_norm):
    depth = ln1.shape[0]
    n_prompt_seq = x_prompt.shape[0]
    x = jnp.concatenate([x_prompt, x_sample], axis=0)
    n_seq, s_len, d = x.shape
    cs_table = _rope_cs_table(s_len)
    for l in range(depth):
        p = _prep_layer_params(dict(
            ln1=ln1[l], w_in=w_in[l], conv_w=conv_w[l], conv_b=conv_b[l], lru_w_a=lru_w_a[l],
            lru_b_a=lru_b_a[l], lru_w_x=lru_w_x[l], lru_b_x=lru_b_x[l], lru_lam=lru_lam[l],
            q_norm=q_norm[l], w_q_up=w_q_up[l], kv_norm=kv_norm[l], w_kv_up=w_kv_up[l],
            q_head_norm=q_head_norm[l], k_head_norm=k_head_norm[l], w_out=w_out[l], ln2=ln2[l],
            w_router=w_router[l], w1=w1[l], w3=w3[l], w2=w2[l]))
        x = _layer(x, n_prompt_seq, p, cs_table)
    y = rms_norm_rows(x.reshape(n_seq * s_len, d), final_norm, _tile(n_seq * s_len, 512))
    y = y.reshape(n_seq, s_len, d)
    return y[:n_prompt_seq], y[n_prompt_seq:]
```

```python
import functools
import math

import jax
import jax.numpy as jnp
from jax import lax
from jax.experimental import pallas as pl
from jax.experimental.pallas import tpu as pltpu

F32 = jnp.float32
BF16 = jnp.bfloat16

EPS = 1e-6
LRU_BLOCKS = 8
CONV_WIDTH = 4
LRU_C = 8.0
V_HEAD_DIM = 128
QK_NOPE = 128
QK_ROPE = 64
QK_HEAD = QK_NOPE + QK_ROPE
ROPE_BASE = 10000.0
N_EXPERTS = 16
CAPACITY = 2

LANES = 128
SUBLANES = 8
HEAD_PAD = 2 * LANES
V7X_VMEM_LIMIT_BYTES = 56 * 1024 * 1024
LOG2E = math.log2(math.e)
COMBINE_STATIC_K = 3


def _tile(n, pref):
    t = min(n, pref)
    while n % t:
        t //= 2
    return t


def _params(*semantics):
    return pltpu.CompilerParams(dimension_semantics=semantics,
                                vmem_limit_bytes=V7X_VMEM_LIMIT_BYTES)


def _rms(x):
    return lax.rsqrt(jnp.mean(x * x, axis=-1, keepdims=True) + EPS)


def _store_row_linear(ref, base, x):
    n_rows, d = x.shape
    pitch = d // LANES
    for c in range(pitch):
        ref[pl.ds(base + c, n_rows, stride=pitch), :] = x[:, c * LANES:(c + 1) * LANES]


def _load_row_linear(ref, base, n_rows, pitch):
    return jnp.concatenate(
        [ref[pl.ds(base + c, n_rows, stride=pitch), :] for c in range(pitch)], axis=1)


def _norm_matmul_kernel(x_ref, g_ref, w_ref, o_ref):
    x = x_ref[...]
    y = x * _rms(x) * g_ref[...]
    o_ref[...] = jnp.dot(y.astype(BF16), w_ref[...], preferred_element_type=F32)


def norm_matmul(x, g, w_bf16, tm):
    rows, d = x.shape
    n = w_bf16.shape[1]
    return pl.pallas_call(
        _norm_matmul_kernel,
        out_shape=jax.ShapeDtypeStruct((rows, n), F32),
        grid=(rows // tm,),
        in_specs=[pl.BlockSpec((tm, d), lambda i: (i, 0)),
                  pl.BlockSpec((1, d), lambda i: (0, 0)),
                  pl.BlockSpec((d, n), lambda i: (0, 0))],
        out_specs=pl.BlockSpec((tm, n), lambda i: (i, 0)),
        compiler_params=_params("parallel"),
        name="norm_matmul",
    )(x, g.reshape(1, d), w_bf16)


def _norm_kernel(x_ref, g_ref, o_ref):
    x = x_ref[...]
    o_ref[...] = x * _rms(x) * g_ref[...]


def rms_norm_rows(x, g, tm):
    rows, d = x.shape
    return pl.pallas_call(
        _norm_kernel,
        out_shape=jax.ShapeDtypeStruct((rows, d), F32),
        grid=(rows // tm,),
        in_specs=[pl.BlockSpec((tm, d), lambda i: (i, 0)),
                  pl.BlockSpec((1, d), lambda i: (0, 0))],
        out_specs=pl.BlockSpec((tm, d), lambda i: (i, 0)),
        compiler_params=_params("parallel"),
        name="final_norm",
    )(x, g.reshape(1, d))


def _lru_kernel(x_ref, prev_ref, next_ref, cw_ref, cb_ref, wg_ref, ba_ref, bx_ref, lam_ref,
                o_ref, *rest, reverse, conv_given):
    if conv_given:
        ext_sc, a_sc, u_sc, carry_sc = rest
    else:
        xc_ref, ext_sc, a_sc, u_sc, carry_sc = rest
    c = pl.program_id(1)
    n_c = pl.num_programs(1)
    t_len, ch = a_sc.shape
    blk = ch // LRU_BLOCKS
    first_in_time = (c == n_c - 1) if reverse else (c == 0)
    last_in_time = (c == 0) if reverse else (c == n_c - 1)

    @pl.when(c == 0)
    def _():
        carry_sc[...] = jnp.zeros(carry_sc.shape, F32)

    if conv_given:
        xc = x_ref[0]
    else:
        ext_sc[0:SUBLANES, :] = jnp.where(first_in_time, 0.0, prev_ref[0])
        ext_sc[SUBLANES:SUBLANES + t_len, :] = x_ref[0]
        ext_sc[SUBLANES + t_len:, :] = jnp.where(last_in_time, 0.0, next_ref[0])
        xc = cb_ref[...]
        for tap in range(CONV_WIDTH):
            off = SUBLANES + tap - 2
            xc = xc + cw_ref[tap:tap + 1, :] * ext_sc[off:off + t_len, :]
        xc_ref[0] = xc

    lam = lam_ref[...]
    log_sig_lam = jnp.minimum(lam, 0.0) - jnp.log1p(jnp.exp(-jnp.abs(lam)))
    for n in range(LRU_BLOCKS):
        sl = slice(n * blk, (n + 1) * blk)
        xcn = xc[:, sl]
        g = jnp.dot(xcn.astype(BF16), wg_ref[n], preferred_element_type=F32)
        r = jax.nn.sigmoid(g[:, :blk] + ba_ref[:, sl])
        i = jax.nn.sigmoid(g[:, blk:] + bx_ref[:, sl])
        log_a = LRU_C * r * log_sig_lam[:, sl]
        a_sc[:, sl] = jnp.exp(log_a)
        u_sc[:, sl] = jnp.sqrt(1.0 - jnp.exp(2.0 * log_a)) * (i * xcn)

    row = lax.broadcasted_iota(jnp.int32, (SUBLANES, ch), 0)
    n_groups = t_len // SUBLANES

    def group(gi, carry):
        g0 = (n_groups - 1 - gi) if reverse else gi
        start = pl.multiple_of(g0 * SUBLANES, SUBLANES)
        a = a_sc[pl.ds(start, SUBLANES), :]
        u = u_sc[pl.ds(start, SUBLANES), :]
        for d in (1, 2, 4):
            shift = SUBLANES - d if reverse else d
            a_s = pltpu.roll(a, shift, 0)
            u_s = pltpu.roll(u, shift, 0)
            valid = (row < SUBLANES - d) if reverse else (row >= d)
            u = jnp.where(valid, a * u_s + u, u)
            a = jnp.where(valid, a * a_s, a)
        h = u + a * carry
        o_ref[0, pl.ds(start, SUBLANES), :] = h
        edge = h[0:1, :] if reverse else h[SUBLANES - 1:SUBLANES, :]
        return jnp.broadcast_to(edge, (SUBLANES, ch))

    carry_sc[...] = lax.fori_loop(0, n_groups, group, carry_sc[...], unroll=2)


def lru_scan(h3, conv_w, conv_b, w_gate, b_a, b_x, lam, t_len, reverse, conv_given=False):
    n_seq, s_len, _ = h3.shape
    ch = conv_w.shape[-1]
    n_c = s_len // t_len
    per = t_len // SUBLANES
    n_sub = s_len // SUBLANES
    cidx = (lambda c: n_c - 1 - c) if reverse else (lambda c: c)
    seq_shape = jax.ShapeDtypeStruct((n_seq, s_len, ch), F32)
    seq_spec = pl.BlockSpec((1, t_len, ch), lambda s, c: (s, cidx(c), 0))
    return pl.pallas_call(
        functools.partial(_lru_kernel, reverse=reverse, conv_given=conv_given),
        out_shape=seq_shape if conv_given else (seq_shape, seq_shape),
        grid=(n_seq, n_c),
        in_specs=[
            pl.BlockSpec((1, t_len, ch), lambda s, c: (s, cidx(c), 0)),
            pl.BlockSpec((1, SUBLANES, ch),
                         lambda s, c: (s, jnp.maximum(cidx(c) * per - 1, 0), 0)),
            pl.BlockSpec((1, SUBLANES, ch),
                         lambda s, c: (s, jnp.minimum((cidx(c) + 1) * per, n_sub - 1), 0)),
            pl.BlockSpec((CONV_WIDTH, ch), lambda s, c: (0, 0)),
            pl.BlockSpec((1, ch), lambda s, c: (0, 0)),
            pl.BlockSpec(w_gate.shape, lambda s, c: (0, 0, 0)),
            pl.BlockSpec((1, ch), lambda s, c: (0, 0)),
            pl.BlockSpec((1, ch), lambda s, c: (0, 0)),
            pl.BlockSpec((1, ch), lambda s, c: (0, 0)),
        ],
        out_specs=seq_spec if conv_given else (seq_spec, seq_spec),
        scratch_shapes=[pltpu.VMEM((t_len + 2 * SUBLANES, ch), F32),
                        pltpu.VMEM((t_len, ch), F32),
                        pltpu.VMEM((t_len, ch), F32),
                        pltpu.VMEM((SUBLANES, ch), F32)],
        compiler_params=_params("parallel", "arbitrary"),
        name="lru_rev" if reverse else "lru_fwd",
    )(h3, h3, h3, conv_w, conv_b.reshape(1, ch), w_gate, b_a.reshape(1, ch),
      b_x.reshape(1, ch), lam.reshape(1, ch))


def _attn_prep_kernel(cq_ref, ckv_ref, kr_ref, cs_ref, qn_ref, kvn_ref, wq_ref, wkv_ref,
                      gq_ref, gk_ref, q_out, k_out, v_out, *, n_heads, q_scale):
    cq = cq_ref[...]
    ckv = ckv_ref[...]
    qa = jnp.dot((cq * _rms(cq) * qn_ref[...]).astype(BF16), wq_ref[...],
                 preferred_element_type=F32)
    kva = jnp.dot((ckv * _rms(ckv) * kvn_ref[...]).astype(BF16), wkv_ref[...],
                  preferred_element_type=F32)
    cs = cs_ref[...]
    lane = lax.broadcasted_iota(jnp.int32, cs.shape, 1)
    rope_lanes = lane < QK_ROPE
    gq = gq_ref[...]
    gk = gk_ref[...]
    vk = kr_ref[...]
    k_rope_ss = 0.5 * jnp.sum(vk * vk, axis=-1, keepdims=True)
    bk = vk * gk[:, LANES:] * cs
    bk = bk + pltpu.roll(bk, QK_ROPE, 1)
    ones = jnp.ones(cs.shape, BF16)
    for h in range(n_heads):
        base = h * HEAD_PAD
        qn = qa[:, base:base + LANES]
        qv = qa[:, base + LANES:base + HEAD_PAD]
        ss = jnp.sum(qn * qn, axis=-1, keepdims=True) + 0.5 * jnp.sum(qv * qv, axis=-1,
                                                                     keepdims=True)
        rs = lax.rsqrt(ss * (1.0 / QK_HEAD) + EPS) * q_scale
        bq = qv * gq[:, LANES:] * cs
        bq = bq + pltpu.roll(bq, QK_ROPE, 1)
        q_out[0, h, :, 0:LANES] = (qn * gq[:, :LANES] * rs).astype(BF16)
        q_out[0, h, :, LANES:] = (bq * rs).astype(BF16)

        kn = kva[:, base:base + LANES]
        ss = jnp.sum(kn * kn, axis=-1, keepdims=True) + k_rope_ss
        rs = lax.rsqrt(ss * (1.0 / QK_HEAD) + EPS)
        k_out[0, h, :, 0:LANES] = (kn * gk[:, :LANES] * rs).astype(BF16)
        k_out[0, h, :, LANES:] = jnp.where(rope_lanes, bk * rs, 0.0).astype(BF16)

        v_out[0, h, :, 0:LANES] = kva[:, base + LANES:base + HEAD_PAD].astype(BF16)
        v_out[0, h, :, LANES:] = ones


def attn_prep(h3, cs_table, q_norm, kv_norm, wq_all, wkv, gq, gk, tm, col_q, col_kv, col_kr):
    n_seq, s_len, _ = h3.shape
    q_lora = q_norm.shape[-1]
    kv_lora = kv_norm.shape[-1]
    n_heads = wq_all.shape[-1] // HEAD_PAD
    q_scale = QK_HEAD ** -0.5 * LOG2E
    out = jax.ShapeDtypeStruct((n_seq, n_heads, s_len, HEAD_PAD), BF16)
    head_spec = pl.BlockSpec((1, n_heads, tm, HEAD_PAD), lambda s, i: (s, 0, i, 0))
    const2 = lambda s, i: (0, 0)
    return pl.pallas_call(
        functools.partial(_attn_prep_kernel, n_heads=n_heads, q_scale=q_scale),
        out_shape=(out, out, out),
        grid=(n_seq, s_len // tm),
        in_specs=[
            pl.BlockSpec((None, tm, q_lora), lambda s, i: (s, i, col_q // q_lora)),
            pl.BlockSpec((None, tm, kv_lora), lambda s, i: (s, i, col_kv // kv_lora)),
            pl.BlockSpec((None, tm, LANES), lambda s, i: (s, i, col_kr // LANES)),
            pl.BlockSpec((tm, LANES), lambda s, i: (i, 0)),
            pl.BlockSpec((1, q_lora), const2),
            pl.BlockSpec((1, kv_lora), const2),
            pl.BlockSpec(wq_all.shape, const2),
            pl.BlockSpec(wkv.shape, const2),
            pl.BlockSpec((1, HEAD_PAD), const2),
            pl.BlockSpec((1, HEAD_PAD), const2),
        ],
        out_specs=(head_spec, head_spec, head_spec),
        compiler_params=_params("parallel", "parallel"),
        name="attn_prep",
    )(h3, h3, h3, cs_table, q_norm.reshape(1, -1), kv_norm.reshape(1, -1), wq_all, wkv, gq, gk)


def _flash_kernel(q_ref, k_ref, v_ref, o_ref, s_a, s_b, p_a, p_b, al_a, al_b, m_sc, acc_sc,
                  *, tk, n_kv):
    q = q_ref[0, 0]
    m_sc[...] = jnp.full(m_sc.shape, -jnp.inf, F32)
    acc_sc[...] = jnp.zeros(acc_sc.shape, F32)

    def scores(j, s_ref):
        k = k_ref[0, 0, j * tk:(j + 1) * tk, :]
        s_ref[...] = lax.dot_general(q, k, (((1,), (1,)), ((), ())),
                                     preferred_element_type=F32)

    def softmax(s_ref, p_ref, al_ref):
        s = s_ref[...]
        m_prev = m_sc[...]
        m_new = jnp.maximum(m_prev, jnp.max(s, axis=-1, keepdims=True))
        al_ref[...] = jnp.exp2(m_prev - m_new)
        p_ref[...] = jnp.exp2(s - m_new[:, :1]).astype(BF16)
        m_sc[...] = m_new

    def values(j, p_ref, al_ref):
        v = v_ref[0, 0, j * tk:(j + 1) * tk, :]
        pv = jnp.dot(p_ref[...], v, preferred_element_type=F32)
        alpha = al_ref[...]
        acc_sc[:, :LANES] = alpha * acc_sc[:, :LANES] + pv[:, :LANES]
        acc_sc[:, LANES:] = alpha * acc_sc[:, LANES:] + pv[:, LANES:]

    bufs = ((s_a, p_a, al_a), (s_b, p_b, al_b))
    scores(0, s_a)
    for j in range(n_kv):
        s_cur, p_cur, al_cur = bufs[j % 2]
        s_nxt, p_prev, al_prev = bufs[(j + 1) % 2]
        if j + 1 < n_kv:
            scores(j + 1, s_nxt)
        softmax(s_cur, p_cur, al_cur)
        if j >= 1:
            values(j - 1, p_prev, al_prev)
    _, p_last, al_last = bufs[(n_kv - 1) % 2]
    values(n_kv - 1, p_last, al_last)
    o_ref[0] = (acc_sc[:, :LANES] / acc_sc[:, LANES:]).astype(o_ref.dtype)


def flash_attention(q, k, v, tq, tk):
    n_seq, n_heads, s_len, dq = q.shape
    n_kv = s_len // tk
    kv_spec = pl.BlockSpec((1, 1, s_len, dq), lambda s, h, i: (s, h, 0, 0))
    return pl.pallas_call(
        functools.partial(_flash_kernel, tk=tk, n_kv=n_kv),
        out_shape=jax.ShapeDtypeStruct((n_seq, s_len, n_heads * V_HEAD_DIM), BF16),
        grid=(n_seq, n_heads, s_len // tq),
        in_specs=[pl.BlockSpec((1, 1, tq, dq), lambda s, h, i: (s, h, i, 0)), kv_spec, kv_spec],
        out_specs=pl.BlockSpec((1, tq, V_HEAD_DIM), lambda s, h, i: (s, i, h)),
        scratch_shapes=[pltpu.VMEM((tq, tk), F32), pltpu.VMEM((tq, tk), F32),
                        pltpu.VMEM((tq, tk), BF16), pltpu.VMEM((tq, tk), BF16),
                        pltpu.VMEM((tq, LANES), F32), pltpu.VMEM((tq, LANES), F32),
                        pltpu.VMEM((tq, LANES), F32), pltpu.VMEM((tq, dq), F32)],
        compiler_params=_params("parallel", "parallel", "arbitrary"),
        name="flash_attention",
    )(q, k, v)


def _mix_out_kernel(hf_ref, hb_ref, y_ref, att_ref, x_ref, w_ref, g_ref, wr_ref,
                    x1_ref, xn_ref, lg_ref):
    ch = hf_ref.shape[-1]
    y = y_ref[...]
    gelu = 0.5 * y * (1.0 + jnp.tanh(math.sqrt(2.0 / math.pi) * (y + 0.044715 * (y * y * y))))
    rnn = ((hf_ref[...] + hb_ref[...]) * gelu).astype(BF16)
    x1 = (x_ref[...]
          + jnp.dot(rnn, w_ref[0:ch, :], preferred_element_type=F32)
          + jnp.dot(att_ref[...], w_ref[ch:, :], preferred_element_type=F32))
    xn = x1 * _rms(x1) * g_ref[...]
    _store_row_linear(x1_ref, 0, x1)
    _store_row_linear(xn_ref, 0, xn)
    wr = wr_ref[...]
    x_hi = xn.astype(BF16)
    x_lo = (xn - x_hi.astype(F32)).astype(BF16)
    both = jnp.dot(x_hi, wr, preferred_element_type=F32)
    lo = jnp.dot(x_lo, wr[:, :LANES], preferred_element_type=F32)
    lg = both[:, :LANES] + both[:, LANES:] + lo
    lg_ref[...] = jnp.transpose(lg)[0:lg_ref.shape[0], :]


def mix_out(hf, hb, h2, att, x, w_out, ln2, w_router_split, n_e, tm):
    rows, ch = hf.shape
    d = x.shape[-1]
    pitch = d // LANES
    row_spec = lambda width: pl.BlockSpec((tm, width), lambda i: (i, 0))
    lin_spec = pl.BlockSpec((tm * pitch, LANES), lambda i: (i, 0))
    lin_shape = jax.ShapeDtypeStruct((rows * pitch, LANES), F32)
    return pl.pallas_call(
        _mix_out_kernel,
        out_shape=(lin_shape, lin_shape, jax.ShapeDtypeStruct((n_e, rows), F32)),
        grid=(rows // tm,),
        in_specs=[row_spec(ch), row_spec(ch),
                  pl.BlockSpec((tm, ch), lambda i: (i, 1)),
                  row_spec(att.shape[-1]), row_spec(d),
                  pl.BlockSpec(w_out.shape, lambda i: (0, 0)),
                  pl.BlockSpec((1, d), lambda i: (0, 0)),
                  pl.BlockSpec(w_router_split.shape, lambda i: (0, 0))],
        out_specs=(lin_spec, lin_spec, pl.BlockSpec((n_e, tm), lambda i: (0, i))),
        compiler_params=_params("parallel"),
        name="mix_out",
    )(hf, hb, h2, att, x, w_out, ln2.reshape(1, d), w_router_split)


def _sum_all(x):
    return jnp.sum(jnp.sum(x, axis=0, keepdims=True), axis=1, keepdims=True)


def _route_kernel(lg_ref, idx_ref, gate_ref, src_ref, cnt_ref, kmax_ref, *, cap, row_offset,
                  slot_offset, s_chunk):
    e = pl.program_id(0)
    n_e, nb, _ = lg_ref.shape
    lg = lg_ref[...]
    m = jnp.max(lg, axis=0)
    den = jnp.sum(jnp.exp(lg - m[None]), axis=0)
    aff = jnp.exp(lg_ref[e] - m) / den
    bits = pltpu.bitcast(aff, jnp.int32)

    def bit_step(i, prefix):
        cand = prefix | jnp.left_shift(jnp.int32(1), 30 - i)
        n_ge = _sum_all(jnp.where(bits >= cand, 1.0, 0.0))
        return jnp.where(n_ge >= cap, cand, prefix)

    thr = lax.fori_loop(0, 31, bit_step, jnp.zeros((1, 1), jnp.int32))

    r128 = lax.broadcasted_iota(jnp.int32, (LANES, LANES), 0)
    c128 = lax.broadcasted_iota(jnp.int32, (LANES, LANES), 1)
    upper_lanes = jnp.where(r128 <= c128, 1.0, 0.0).astype(BF16)
    rb = lax.broadcasted_iota(jnp.int32, (nb, nb), 0)
    cb = lax.broadcasted_iota(jnp.int32, (nb, nb), 1)
    lower_rows = jnp.where(cb < rb, 1.0, 0.0).astype(BF16)
    upper_rows = jnp.where(rb <= cb, 1.0, 0.0).astype(BF16)

    def cumsum_tokens(mf):
        row_cum = jnp.dot(mf.astype(BF16), upper_lanes, preferred_element_type=F32)
        tot = jnp.broadcast_to(row_cum[:, LANES - 1:LANES], (nb, LANES))
        return row_cum + jnp.dot(lower_rows, tot.astype(BF16), preferred_element_type=F32)

    gt = bits > thr
    eq = bits == thr
    eq_f = jnp.where(eq, 1.0, 0.0)
    need = cap - _sum_all(jnp.where(gt, 1.0, 0.0))
    eq_rank = cumsum_tokens(eq_f) - eq_f
    sel = gt | (eq & (eq_rank < need))
    sel_f = jnp.where(sel, 1.0, 0.0)
    cum = cumsum_tokens(sel_f)

    @pl.when(e == 0)
    def _():
        cnt_ref[...] = jnp.zeros(cnt_ref.shape, jnp.int32)
        src_ref[...] = jnp.zeros(src_ref.shape, jnp.int32)

    rank = cnt_ref[...]
    flat = (cum - sel_f).astype(jnp.int32) + (slot_offset + e * cap)
    for k in range(n_e):
        src_ref[k] = jnp.where(sel & (rank == k), flat, src_ref[k])
    cnt_new = rank + sel.astype(jnp.int32)
    cnt_ref[...] = cnt_new
    row_max = jnp.max(cnt_new.astype(F32), axis=1, keepdims=True)
    kmax_ref[...] = jnp.broadcast_to(row_max, kmax_ref.shape).astype(jnp.int32)

    tot_lane = lax.dot_general(jnp.ones((SUBLANES, LANES), BF16), sel_f.astype(BF16),
                               (((1,), (1,)), ((), ())), preferred_element_type=F32)
    end_incl = jnp.dot(tot_lane.astype(BF16), upper_rows, preferred_element_type=F32)[0:1]
    end_excl = end_incl - tot_lane[0:1]
    cum_hi = jnp.floor(cum * (1.0 / 64.0))
    a1 = aff.astype(BF16)
    r1 = aff - a1.astype(F32)
    a2 = r1.astype(BF16)
    a3 = (r1 - a2.astype(F32)).astype(BF16)
    payload = jnp.concatenate(
        [cum_hi.astype(BF16), (cum - 64.0 * cum_hi).astype(BF16), a1, a2, a3], axis=1)
    row_id = lax.broadcasted_iota(jnp.int32, (s_chunk, nb), 1).astype(F32)
    lane_id = lax.broadcasted_iota(jnp.int32, (s_chunk, LANES), 1)

    def chunk(c, carry):
        base = pl.multiple_of(c * s_chunk, s_chunk)
        s_col = (lax.broadcasted_iota(jnp.int32, (s_chunk, 1), 0) + base).astype(F32)
        hit = (end_excl <= s_col) & (s_col < end_incl)
        hit_f = jnp.where(hit, 1.0, 0.0)
        got = jnp.dot(hit_f.astype(BF16), payload, preferred_element_type=F32)
        cum_row = 64.0 * got[:, 0:LANES] + got[:, LANES:2 * LANES]
        aff_row = got[:, 2 * LANES:3 * LANES] + got[:, 3 * LANES:4 * LANES] + got[:, 4 * LANES:]
        j = jnp.sum(jnp.where(cum_row <= s_col, 1.0, 0.0), axis=1, keepdims=True)
        blk = jnp.sum(hit_f * row_id, axis=1, keepdims=True)
        idx_ref[0, pl.ds(base, s_chunk), :] = (blk * LANES + j).astype(jnp.int32) + row_offset
        gate_ref[0, pl.ds(base, s_chunk), :] = jnp.sum(
            jnp.where(lane_id == j.astype(jnp.int32), aff_row, 0.0), axis=1, keepdims=True)
        return carry

    lax.fori_loop(0, cap // s_chunk, chunk, 0)


def route(lg, cap, row_offset, slot_offset):
    n_e, nb, _ = lg.shape
    assert cap <= 64 * 256, "slot counts are split into two bf16-exact digits"
    s_chunk = _tile(cap, 512)
    return pl.pallas_call(
        functools.partial(_route_kernel, cap=cap, row_offset=row_offset,
                          slot_offset=slot_offset, s_chunk=s_chunk),
        out_shape=(jax.ShapeDtypeStruct((n_e, cap, 1), jnp.int32),
                   jax.ShapeDtypeStruct((n_e, cap, 1), F32),
                   jax.ShapeDtypeStruct((n_e, nb, LANES), jnp.int32),
                   jax.ShapeDtypeStruct((nb, LANES), jnp.int32),
                   jax.ShapeDtypeStruct((nb, LANES), jnp.int32)),
        grid=(n_e,),
        in_specs=[pl.BlockSpec((n_e, nb, LANES), lambda e: (0, 0, 0))],
        out_specs=(pl.BlockSpec((1, cap, 1), lambda e: (e, 0, 0)),
                   pl.BlockSpec((1, cap, 1), lambda e: (e, 0, 0)),
                   pl.BlockSpec((n_e, nb, LANES), lambda e: (0, 0, 0)),
                   pl.BlockSpec((nb, LANES), lambda e: (0, 0)),
                   pl.BlockSpec((nb, LANES), lambda e: (0, 0))),
        compiler_params=_params("arbitrary"),
        name="route",
    )(lg)


def _ffn_gather_kernel(idx_ref, idx_next_ref, gate_ref, xn_hbm, w1_ref, w3_ref, w2_ref,
                       o_ref, xbuf, xs, acc, sem, *, n_f):
    t = pl.program_id(0)
    f = pl.program_id(1)
    n_t = pl.num_programs(0)
    tm, d = xs.shape
    pitch = d // LANES
    slot = t % 2
    rows_per_step = tm // n_f

    def row_copy(idx_smem, r, slot_):
        src = pl.multiple_of(idx_smem[0, 0, r] * pitch, pitch)
        dst = pl.multiple_of((slot_ * tm + r) * pitch, pitch)
        return pltpu.make_async_copy(xn_hbm.at[pl.ds(src, pitch), :],
                                     xbuf.at[pl.ds(dst, pitch), :], sem.at[slot_])

    def start_rows(idx_smem, slot_):
        def body(r, carry):
            row_copy(idx_smem, r, slot_).start()
            return carry
        lax.fori_loop(0, tm, body, 0, unroll=8)

    def wait_slot(slot_):
        base = pl.multiple_of(slot_ * (tm * pitch), SUBLANES)
        pltpu.make_async_copy(xn_hbm.at[pl.ds(0, tm * pitch), :],
                              xbuf.at[pl.ds(base, tm * pitch), :], sem.at[slot_]).wait()

    @pl.when((t == 0) & (f == 0))
    def _():
        start_rows(idx_ref, 0)

    for r in range(rows_per_step):
        row_copy(idx_next_ref, f * rows_per_step + r, 1 - slot).start()

    @pl.when(f == 0)
    def _():
        wait_slot(slot)
        base = pl.multiple_of(slot * (tm * pitch), SUBLANES)
        xs[...] = _load_row_linear(xbuf, base, tm, pitch).astype(BF16)

    x = xs[...]
    a = jnp.dot(x, w1_ref[0], preferred_element_type=F32)
    b = jnp.dot(x, w3_ref[0], preferred_element_type=F32)
    hid = (a * jax.nn.sigmoid(a)) * b
    y = jnp.dot(hid.astype(BF16), w2_ref[0], preferred_element_type=F32)

    @pl.when(f == 0)
    def _():
        acc[...] = y

    @pl.when(f > 0)
    def _():
        acc[...] += y

    @pl.when(f == n_f - 1)
    def _():
        _store_row_linear(o_ref, 0, acc[...] * gate_ref[0])

        @pl.when(t == n_t - 1)
        def _():
            wait_slot(1 - slot)


def expert_ffn_gather(idx_tiles, gate_tiles, xn_lin, d, w1, w3, w2, tile_expert, tf):
    n_tiles, _, tm = idx_tiles.shape
    pitch = d // LANES
    d_ff = w1.shape[-1]
    n_f = d_ff // tf
    assert tm % n_f == 0
    smem = pltpu.MemorySpace.SMEM
    return pl.pallas_call(
        functools.partial(_ffn_gather_kernel, n_f=n_f),
        out_shape=jax.ShapeDtypeStruct((n_tiles * tm * pitch, LANES), F32),
        grid=(n_tiles, d_ff // tf),
        in_specs=[pl.BlockSpec((1, 1, tm), lambda t, f: (t, 0, 0), memory_space=smem),
                  pl.BlockSpec((1, 1, tm), lambda t, f: (jnp.minimum(t + 1, n_tiles - 1), 0, 0),
                               memory_space=smem),
                  pl.BlockSpec((1, tm, 1), lambda t, f: (t, 0, 0)),
                  pl.BlockSpec(memory_space=pl.ANY),
                  pl.BlockSpec((1, d, tf), lambda t, f: (tile_expert(t), 0, f)),
                  pl.BlockSpec((1, d, tf), lambda t, f: (tile_expert(t), 0, f)),
                  pl.BlockSpec((1, tf, d), lambda t, f: (tile_expert(t), f, 0))],
        out_specs=pl.BlockSpec((tm * pitch, LANES), lambda t, f: (t, 0)),
        scratch_shapes=[pltpu.VMEM((2 * tm * pitch, LANES), F32),
                        pltpu.VMEM((tm, d), BF16),
                        pltpu.VMEM((tm, d), F32),
                        pltpu.SemaphoreType.DMA((2,))],
        compiler_params=_params("arbitrary", "arbitrary"),
        name="expert_ffn",
    )(idx_tiles, idx_tiles, gate_tiles, xn_lin, w1, w3, w2)


def _combine_kernel(cnt_ref, src_ref, cnt_next_ref, src_next_ref, x1_ref, ye_hbm, o_ref,
                    buf, acc, sem, meta):
    i = pl.program_id(0)
    n_i = pl.num_programs(0)
    slot = i % 2
    n_tok, d = o_ref.shape
    pitch = d // LANES
    n_e = src_ref.shape[1]
    slab = n_tok * pitch

    def row_copy(row, k, t, slot_):
        src = pl.multiple_of(row * pitch, pitch)
        dst = pl.multiple_of(((slot_ * n_e + k) * n_tok + t) * pitch, pitch)
        return pltpu.make_async_copy(ye_hbm.at[pl.ds(src, pitch), :],
                                     buf.at[pl.ds(dst, pitch), :], sem.at[slot_])

    def slab_start(k, slot_):
        return pl.multiple_of((slot_ * n_e + k) * slab, SUBLANES)

    def fetch(cnt_smem, src_smem, slot_):
        k_max = cnt_smem[0, 1, 0]

        def clear(k, carry):
            buf[pl.ds(slab_start(k, slot_), slab), :] = jnp.zeros((slab, LANES), F32)
            return carry
        lax.fori_loop(0, k_max, clear, 0)

        def token(t, total):
            c = cnt_smem[0, 0, t]

            def one(k, carry):
                row_copy(src_smem[0, k, t], k, t, slot_).start()
                return carry

            for k in range(COMBINE_STATIC_K):
                @pl.when(c > k)
                def _():
                    one(k, 0)

            @pl.when(c > COMBINE_STATIC_K)
            def _():
                lax.fori_loop(COMBINE_STATIC_K, c, one, 0)
            return total + c
        meta[slot_, 0] = lax.fori_loop(0, n_tok, token, 0)
        meta[slot_, 1] = k_max

    @pl.when(i == 0)
    def _():
        fetch(cnt_ref, src_ref, 0)

    @pl.when(i + 1 < n_i)
    def _():
        fetch(cnt_next_ref, src_next_ref, 1 - slot)

    total = meta[slot, 0]
    for bit in range((n_e * n_tok).bit_length()):
        @pl.when(((total >> bit) & 1) == 1)
        def _():
            n_rows = (1 << bit) * pitch
            pltpu.make_async_copy(ye_hbm.at[pl.ds(0, n_rows), :],
                                  buf.at[pl.ds(slab_start(0, slot), n_rows), :],
                                  sem.at[slot]).wait()

    acc[...] = x1_ref[...]

    def add(k, carry):
        acc[...] += buf[pl.ds(slab_start(k, slot), slab), :]
        return carry
    lax.fori_loop(0, meta[slot, 1], add, 0)
    o_ref[...] = _load_row_linear(acc, 0, n_tok, pitch)


def combine(cnt_blocks, src_blocks, x1_lin, ye_lin, d):
    n_blk, n_e, n_tok = src_blocks.shape
    pitch = d // LANES
    smem = pltpu.MemorySpace.SMEM
    nxt = lambda i: jnp.minimum(i + 1, n_blk - 1)
    return pl.pallas_call(
        _combine_kernel,
        out_shape=jax.ShapeDtypeStruct((n_blk * n_tok, d), F32),
        grid=(n_blk,),
        in_specs=[pl.BlockSpec((1, 2, n_tok), lambda i: (i, 0, 0), memory_space=smem),
                  pl.BlockSpec((1, n_e, n_tok), lambda i: (i, 0, 0), memory_space=smem),
                  pl.BlockSpec((1, 2, n_tok), lambda i: (nxt(i), 0, 0), memory_space=smem),
                  pl.BlockSpec((1, n_e, n_tok), lambda i: (nxt(i), 0, 0), memory_space=smem),
                  pl.BlockSpec((n_tok * pitch, LANES), lambda i: (i, 0)),
                  pl.BlockSpec(memory_space=pl.ANY)],
        out_specs=pl.BlockSpec((n_tok, d), lambda i: (i, 0)),
        scratch_shapes=[pltpu.VMEM((2 * n_e * n_tok * pitch, LANES), F32),
                        pltpu.VMEM((n_tok * pitch, LANES), F32),
                        pltpu.SemaphoreType.DMA((2,)),
                        pltpu.SMEM((2, 2), jnp.int32)],
        compiler_params=_params("arbitrary"),
        name="combine",
    )(cnt_blocks, src_blocks, cnt_blocks, src_blocks, x1_lin, ye_lin)


def _swap_halves(w):
    half = w.shape[-1] // 2
    return jnp.concatenate([w[..., half:], w[..., :half]], axis=-1)


def _prep_layer_params(p):
    d_lru = p["conv_w"].shape[-1]
    q_lora = p["q_norm"].shape[-1]
    kv_lora = p["kv_norm"].shape[-1]
    n_heads = p["w_q_up"].shape[-1] // QK_HEAD
    s3 = 2 * d_lru + q_lora + kv_lora
    w_in = p["w_in"]
    w_in_ext = jnp.concatenate([w_in, _swap_halves(w_in[:, s3:])], axis=-1).astype(BF16)
    wq = p["w_q_up"].reshape(q_lora, n_heads, QK_HEAD)
    wq_rope = wq[..., QK_NOPE:]
    wq_all = jnp.concatenate([wq[..., :QK_NOPE], wq_rope, _swap_halves(wq_rope)], axis=-1)
    wq_all = wq_all.reshape(q_lora, n_heads * HEAD_PAD).astype(BF16)

    def head_gain(g):
        g_rope = g[QK_NOPE:]
        return jnp.concatenate([g[:QK_NOPE], g_rope, _swap_halves(g_rope)]).reshape(1, HEAD_PAD)

    w_r = p["w_router"]
    w_r_hi = w_r.astype(BF16)
    w_r_lo = (w_r - w_r_hi.astype(F32)).astype(BF16)
    lane_pad = ((0, 0), (0, LANES - w_r.shape[-1]))
    w_router_split = jnp.concatenate([jnp.pad(w_r_hi, lane_pad), jnp.pad(w_r_lo, lane_pad)],
                                     axis=-1)

    return dict(
        w_router_split=w_router_split,
        ln1=p["ln1"], w_in_ext=w_in_ext, conv_w=p["conv_w"], conv_b=p["conv_b"],
        w_gate=[jnp.concatenate([p["lru_w_a"][r], p["lru_w_x"][r]], axis=-1).astype(BF16)
                for r in range(2)],
        lru_b_a=p["lru_b_a"], lru_b_x=p["lru_b_x"], lru_lam=p["lru_lam"],
        q_norm=p["q_norm"], kv_norm=p["kv_norm"], wq_all=wq_all,
        wkv=p["w_kv_up"].astype(BF16), gq=head_gain(p["q_head_norm"]),
        gk=head_gain(p["k_head_norm"]), w_out=p["w_out"].astype(BF16), ln2=p["ln2"],
        w1=p["w1"].astype(BF16), w3=p["w3"].astype(BF16),
        w2=p["w2"].astype(BF16), col_q=2 * d_lru, col_kv=2 * d_lru + q_lora, col_kr=s3)


def _rope_cs_table(s_len):
    pos = jnp.arange(s_len, dtype=F32)
    inv = ROPE_BASE ** (-jnp.arange(0, QK_ROPE, 2, dtype=F32) / QK_ROPE)
    ang = pos[:, None] * inv[None, :]
    cos, sin = jnp.cos(ang), jnp.sin(ang)
    return jnp.concatenate([cos, cos, -sin, sin], axis=-1)


def _layer(x, n_prompt_seq, p, cs_table):
    n_seq, s_len, d = x.shape
    rows = n_seq * s_len
    d_lru = p["conv_w"].shape[-1]

    h = norm_matmul(x.reshape(rows, d), p["ln1"], p["w_in_ext"], tm=_tile(rows, 256))
    h3 = h.reshape(n_seq, s_len, -1)

    t_len = _tile(s_len, 512)
    h_fwd, xc3 = lru_scan(h3, p["conv_w"], p["conv_b"], p["w_gate"][0], p["lru_b_a"][0],
                          p["lru_b_x"][0], p["lru_lam"][0], t_len, reverse=False)
    h_rev = lru_scan(xc3, p["conv_w"], p["conv_b"], p["w_gate"][1], p["lru_b_a"][1],
                     p["lru_b_x"][1], p["lru_lam"][1], t_len, reverse=True, conv_given=True)
    h_dir = [h_fwd, h_rev]

    q, k, v = attn_prep(h3, cs_table, p["q_norm"], p["kv_norm"], p["wq_all"], p["wkv"],
                        p["gq"], p["gk"], _tile(s_len, 512), p["col_q"], p["col_kv"],
                        p["col_kr"])
    att = flash_attention(q, k, v, tq=_tile(s_len, 1024), tk=_tile(s_len, 1024))

    x1_lin, xn_lin, logits_t = mix_out(
        h_dir[0].reshape(rows, d_lru), h_dir[1].reshape(rows, d_lru), h,
        att.reshape(rows, -1), x.reshape(rows, d), p["w_out"], p["ln2"], p["w_router_split"],
        N_EXPERTS, tm=_tile(rows, 256))

    bounds = (0, n_prompt_seq * s_len, rows)
    caps = [max(1, (CAPACITY * (hi - lo)) // N_EXPERTS) for lo, hi in zip(bounds, bounds[1:])]
    tm = _tile(min(caps), 512)
    idx_l, gate_l, src_l, cnt_l = [], [], [], []
    slot_offset = 0
    for (lo, hi), cap in zip(zip(bounds, bounds[1:]), caps):
        lg = logits_t[:, lo:hi].reshape(N_EXPERTS, (hi - lo) // LANES, LANES)
        idx, gate, src, cnt, kmax = route(lg, cap, row_offset=lo, slot_offset=slot_offset)
        slot_offset += N_EXPERTS * cap
        idx_l.append(idx.reshape(-1, 1, tm))
        gate_l.append(gate.reshape(-1, tm, 1))
        src_l.append(src.transpose(1, 0, 2))
        cnt_l.append(jnp.stack([cnt, kmax], axis=1))
    first_tiles = N_EXPERTS * (caps[0] // tm)

    def tile_expert(t):
        return jnp.where(t < first_tiles, t // (caps[0] // tm),
                         (t - first_tiles) // (caps[1] // tm))

    ye_lin = expert_ffn_gather(jnp.concatenate(idx_l), jnp.concatenate(gate_l), xn_lin, d,
                               p["w1"], p["w3"], p["w2"], tile_expert, tf=_tile(d, 1024))
    x2 = combine(jnp.concatenate(cnt_l), jnp.concatenate(src_l), x1_lin, ye_lin, d)
    return x2.reshape(n_seq, s_len, d)


def kernel(x_prompt, x_sample, ln1, w_in, conv_w, conv_b, lru_w_a, lru_b_a, lru_w_x, lru_b_x,
           lru_lam, q_norm, w_q_up, kv_norm, w_kv_up, q_head_norm, k_head_norm, w_out, ln2,
           w_router, w1, w3, w2, final_norm):
    depth = ln1.shape[0]
    n_prompt_seq = x_prompt.shape[0]
    x = jnp.concatenate([x_prompt, x_sample], axis=0)
    n_seq, s_len, d = x.shape
    cs_table = _rope_cs_table(s_len)
    for l in range(depth):
        p = _prep_layer_params(dict(
            ln1=ln1[l], w_in=w_in[l], conv_w=conv_w[l], conv_b=conv_b[l], lru_w_a=lru_w_a[l],
            lru_b_a=lru_b_a[l], lru_w_x=lru_w_x[l], lru_b_x=lru_b_x[l], lru_lam=lru_lam[l],
            q_norm=q_norm[l], w_q_up=w_q_up[l], kv_norm=kv_norm[l], w_kv_up=w_kv_up[l],
            q_head_norm=q_head_norm[l], k_head_norm=k_head_norm[l], w_out=w_out[l], ln2=ln2[l],
            w_router=w_router[l], w1=w1[l], w3=w3[l], w2=w2[l]))
        x = _layer(x, n_prompt_seq, p, cs_table)
    y = rms_norm_rows(x.reshape(n_seq * s_len, d), final_norm, _tile(n_seq * s_len, 512))
    y = y.reshape(n_seq, s_len, d)
    return y[:n_prompt_seq], y[n_prompt_seq:]
```
